```python
import jax, jax.numpy as jnp
from jax import lax
import numpy as np

D_MODEL = 1024
BATCH = 16
SEQ = 4096
DEPTH = 2

POOL_WIDTH = D_MODEL // 2
POOL_WINDOWS = (2, 4, 8, 16)
N_POOL_GROUPS = len(POOL_WINDOWS)
POOL_GROUP = POOL_WIDTH // N_POOL_GROUPS
ATTN_WIDTH = D_MODEL - POOL_WIDTH
HEAD_DIM = 64
N_HEADS = ATTN_WIDTH // HEAD_DIM
N_KV_GROUPS = 2
HEADS_PER_GROUP = N_HEADS // N_KV_GROUPS
KV_WIDTH = N_KV_GROUPS * HEAD_DIM
N_BRANCHES = 3
CMP_LEN = 32
CMP_STRIDE = 16
CMP_HIDDEN = 2 * HEAD_DIM
SEL_BLOCK = 64
SEL_TOPK = 16
WINDOW = 512
Q_CHUNK = 32
D_FF = 2816
CONV_WIDTH = 3
EPS = 1e-6
NEG_INF = -1e30
FORCE_SCORE = 1e6
IN_SIZES = [POOL_WIDTH, ATTN_WIDTH] + [KV_WIDTH] * (2 * N_BRANCHES) + [N_HEADS * N_BRANCHES]
IN_WIDTH = sum(IN_SIZES)
SPLIT_POINTS = [int(v) for v in np.cumsum(IN_SIZES)[:-1]]

kernel_name = 'hybrid_pool_nsa_convffn'


def rms_norm(x, g):
    xf = x.astype(jnp.float32)
    y = xf * lax.rsqrt(jnp.mean(xf * xf, axis=-1, keepdims=True) + EPS)
    return (y * g.astype(jnp.float32)).astype(x.dtype)


def masked_softmax(s, mask):
    return jax.nn.softmax(jnp.where(mask, s, NEG_INF), axis=-1) * mask


def pool_mixer(u, w_pool, s_pool):
    B, T, _ = u.shape
    uf = u.astype(jnp.float32).reshape(B, T, N_POOL_GROUPS, POOL_GROUP)
    csum = jnp.concatenate([jnp.zeros_like(uf[:, :1]), jnp.cumsum(uf, axis=1)], axis=1)
    pos = jnp.arange(T)
    groups = []
    for gi, w in enumerate(POOL_WINDOWS):
        c = csum[:, :, gi]
        lo = jnp.maximum(pos + 1 - w, 0)
        cnt = jnp.minimum(pos + 1, w).astype(jnp.float32)[None, :, None]
        groups.append((c[:, 1:] - c[:, lo]) / cnt)
    pooled = jnp.stack(groups, axis=2) - uf
    mixed = jnp.einsum('btpc,pcd->btpd', pooled, w_pool.astype(jnp.float32))
    return (mixed.reshape(B, T, POOL_WIDTH) * s_pool.astype(jnp.float32)).astype(u.dtype)


def compress(a, pe, w1, w2):
    B, T = a.shape[0], a.shape[1]
    n_cmp = (T - CMP_LEN) // CMP_STRIDE + 1
    idx = jnp.arange(n_cmp)[:, None] * CMP_STRIDE + jnp.arange(CMP_LEN)[None, :]
    blocks = a[:, idx] + pe[None, None, :, None, :]
    blocks = blocks.transpose(0, 1, 3, 2, 4).reshape(B, n_cmp, N_KV_GROUPS, CMP_LEN * HEAD_DIM)
    return jax.nn.gelu(blocks @ w1) @ w2


def nsa_attention(q, k_cmp, v_cmp, k_sel, v_sel, k_win, v_win, gates):
    B, T = q.shape[0], q.shape[1]
    n_cmp = k_cmp.shape[1]
    n_sel = T // SEL_BLOCK
    topk = min(SEL_TOPK, n_sel)
    scale = HEAD_DIM ** -0.5
    cmp_start = jnp.arange(n_cmp) * CMP_STRIDE
    cmp_end = cmp_start + CMP_LEN - 1
    blk = jnp.arange(n_sel)
    blk_start = blk * SEL_BLOCK
    overlap = ((cmp_start[:, None] < blk_start[None, :] + SEL_BLOCK)
               & (cmp_end[:, None] >= blk_start[None, :])).astype(jnp.float32)

    def to_blocks(a):
        a = a.reshape(B, n_sel, SEL_BLOCK, N_KV_GROUPS, HEAD_DIM).transpose(0, 3, 1, 2, 4)
        return a.reshape(B * N_KV_GROUPS * n_sel, SEL_BLOCK, HEAD_DIM)

    kb, vb = to_blocks(k_sel), to_blocks(v_sel)
    base = (jnp.arange(B)[:, None] * N_KV_GROUPS + jnp.arange(N_KV_GROUPS)[None, :]) * n_sel
    pad = ((0, 0), (WINDOW, 0), (0, 0), (0, 0))
    kw_pad, vw_pad = jnp.pad(k_win, pad), jnp.pad(v_win, pad)
    in_blk = jnp.arange(SEL_BLOCK)
    win_off = jnp.arange(WINDOW + Q_CHUNK) - WINDOW

    def one_chunk(start):
        t = start + jnp.arange(Q_CHUNK)
        qc = lax.dynamic_slice_in_dim(q, start, Q_CHUNK, axis=1)
        gc = lax.dynamic_slice_in_dim(gates, start, Q_CHUNK, axis=1)
        s_c = jnp.einsum('bqgrd,bngd->bgrqn', qc, k_cmp).astype(jnp.float32) * scale
        p_c = masked_softmax(s_c, cmp_end[None, :] <= t[:, None])
        o_c = jnp.einsum('bgrqn,bngd->bqgrd', p_c.astype(v_cmp.dtype), v_cmp)
        imp = jnp.einsum('bgrqn,nj->bgqj', p_c, overlap)
        cur = t // SEL_BLOCK
        forced = (blk[None, :] == 0) | (blk[None, :] == cur[:, None]) | (blk[None, :] == cur[:, None] - 1)
        imp = jnp.where(forced, FORCE_SCORE, imp)
        imp = jnp.where(blk_start[None, :] <= t[:, None], imp, NEG_INF)
        _, idx = lax.top_k(imp, topk)
        flat = base[:, :, None, None] + idx
        kg = jnp.take(kb, flat, axis=0)
        vg = jnp.take(vb, flat, axis=0)
        kpos = idx[..., None] * SEL_BLOCK + in_blk
        m_s = (kpos <= t[None, None, :, None, None]).reshape(B, N_KV_GROUPS, 1, Q_CHUNK, topk * SEL_BLOCK)
        s_s = jnp.einsum('bqgrd,bgqksd->bgrqks', qc, kg).astype(jnp.float32) * scale
        p_s = masked_softmax(s_s.reshape(B, N_KV_GROUPS, HEADS_PER_GROUP, Q_CHUNK, topk * SEL_BLOCK), m_s)
        p_s = p_s.reshape(s_s.shape)
        o_s = jnp.einsum('bgrqks,bgqksd->bqgrd', p_s.astype(vg.dtype), vg)
        kw = lax.dynamic_slice_in_dim(kw_pad, start, WINDOW + Q_CHUNK, axis=1)
        vw = lax.dynamic_slice_in_dim(vw_pad, start, WINDOW + Q_CHUNK, axis=1)
        kpos_w = start + win_off
        m_w = ((kpos_w[None, :] <= t[:, None]) & (kpos_w[None, :] > t[:, None] - WINDOW)
               & (kpos_w[None, :] >= 0))
        s_w = jnp.einsum('bqgrd,bkgd->bgrqk', qc, kw).astype(jnp.float32) * scale
        p_w = masked_softmax(s_w, m_w)
        o_w = jnp.einsum('bgrqk,bkgd->bqgrd', p_w.astype(vw.dtype), vw)
        o = gc[..., 0:1] * o_c + gc[..., 1:2] * o_s + gc[..., 2:3] * o_w
        return o.reshape(B, Q_CHUNK, ATTN_WIDTH)

    starts = jnp.arange(T // Q_CHUNK) * Q_CHUNK
    out = lax.map(one_chunk, starts)
    return out.transpose(1, 0, 2, 3).reshape(B, T, ATTN_WIDTH)


def conv_ffn(h, w_gate, w_up, conv_w, conv_b, w_down):
    g = h @ w_gate
    g = lax.conv_general_dilated(g, conv_w[:, None, :].astype(g.dtype), window_strides=(1,),
                                 padding=[(CONV_WIDTH - 1, 0)],
                                 dimension_numbers=('NWC', 'WIO', 'NWC'),
                                 feature_group_count=D_FF) + conv_b
    return (jax.nn.silu(g) * (h @ w_up)) @ w_down


def setup_inputs(seed: int = 0) -> dict:
    key = jax.random.key(seed)
    ks = jax.random.split(key, 24)
    L = DEPTH

    def nrm(k, shape, scale):
        return jax.random.normal(k, shape, jnp.float32) * scale

    def gain(k, shape):
        return 1.0 + 0.02 * jax.random.normal(k, shape, jnp.float32)

    return {
        'x': nrm(ks[0], (BATCH, SEQ, D_MODEL), 1.0),
        'norm1': gain(ks[1], (L, D_MODEL)),
        'w_in': nrm(ks[2], (L, D_MODEL, IN_WIDTH), D_MODEL ** -0.5),
        'w_pool': nrm(ks[3], (L, N_POOL_GROUPS, POOL_GROUP, POOL_GROUP), POOL_GROUP ** -0.5),
        's_pool': gain(ks[4], (L, POOL_WIDTH)),
        'cmp_pe_k': nrm(ks[5], (L, CMP_LEN, HEAD_DIM), 0.1),
        'cmp_w1_k': nrm(ks[6], (L, CMP_LEN * HEAD_DIM, CMP_HIDDEN), (CMP_LEN * HEAD_DIM) ** -0.5),
        'cmp_w2_k': nrm(ks[7], (L, CMP_HIDDEN, HEAD_DIM), CMP_HIDDEN ** -0.5),
        'cmp_pe_v': nrm(ks[8], (L, CMP_LEN, HEAD_DIM), 0.1),
        'cmp_w1_v': nrm(ks[9], (L, CMP_LEN * HEAD_DIM, CMP_HIDDEN), (CMP_LEN * HEAD_DIM) ** -0.5),
        'cmp_w2_v': nrm(ks[10], (L, CMP_HIDDEN, HEAD_DIM), CMP_HIDDEN ** -0.5),
        'norm_pool_out': gain(ks[11], (L, POOL_WIDTH)),
        'norm_attn_out': gain(ks[12], (L, ATTN_WIDTH)),
        'w_out': nrm(ks[13], (L, D_MODEL, D_MODEL), D_MODEL ** -0.5),
        'norm2': gain(ks[14], (L, D_MODEL)),
        'w_gate': nrm(ks[15], (L, D_MODEL, D_FF), D_MODEL ** -0.5),
        'w_up': nrm(ks[16], (L, D_MODEL, D_FF), D_MODEL ** -0.5),
        'conv_w': nrm(ks[17], (L, CONV_WIDTH, D_FF), CONV_WIDTH ** -0.5),
        'conv_b': nrm(ks[18], (L, D_FF), 0.01),
        'w_down': nrm(ks[19], (L, D_FF, D_MODEL), D_FF ** -0.5),
        'norm_f': gain(ks[20], (D_MODEL,)),
    }


def reference(x, norm1, w_in, w_pool, s_pool, cmp_pe_k, cmp_w1_k, cmp_w2_k, cmp_pe_v, cmp_w1_v, cmp_w2_v,
              norm_pool_out, norm_attn_out, w_out, norm2, w_gate, w_up, conv_w, conv_b, w_down, norm_f):
    B, T, _ = x.shape

    def kv(a):
        return a.reshape(B, T, N_KV_GROUPS, HEAD_DIM)

    for l in range(DEPTH):
        h = rms_norm(x, norm1[l])
        z = h @ w_in[l]
        u, q, kc, vc, ksel, vsel, kwin, vwin, gl = jnp.split(z, SPLIT_POINTS, axis=-1)
        pool_out = pool_mixer(u, w_pool[l], s_pool[l])
        k_cmp = compress(kv(kc), cmp_pe_k[l], cmp_w1_k[l], cmp_w2_k[l])
        v_cmp = compress(kv(vc), cmp_pe_v[l], cmp_w1_v[l], cmp_w2_v[l])
        gates = jax.nn.sigmoid(gl.astype(jnp.float32)).astype(x.dtype)
        gates = gates.reshape(B, T, N_KV_GROUPS, HEADS_PER_GROUP, N_BRANCHES)
        attn_out = nsa_attention(q.reshape(B, T, N_KV_GROUPS, HEADS_PER_GROUP, HEAD_DIM),
                                 k_cmp, v_cmp, kv(ksel), kv(vsel), kv(kwin), kv(vwin), gates)
        mixed = jnp.concatenate([rms_norm(pool_out, norm_pool_out[l]),
                                 rms_norm(attn_out, norm_attn_out[l])], axis=-1)
        x = x + mixed @ w_out[l]
        x = x + conv_ffn(rms_norm(x, norm2[l]), w_gate[l], w_up[l], conv_w[l], conv_b[l], w_down[l])
    return rms_norm(x, norm_f)
```

```python
import functools

import jax
import jax.numpy as jnp
import numpy as np
from jax import lax
from jax.experimental import pallas as pl
from jax.experimental.pallas import tpu as pltpu

D_MODEL = 1024
POOL_WIDTH = 512
POOL_WINDOWS = (2, 4, 8, 16)
POOL_GROUP = 128
ATTN_WIDTH = 512
HEAD_DIM = 64
N_KV_GROUPS = 2
HEADS_PER_GROUP = 4
GROUP_WIDTH = HEADS_PER_GROUP * HEAD_DIM
KV_WIDTH = N_KV_GROUPS * HEAD_DIM
N_BRANCHES = 3
CMP_LEN = 32
CMP_STRIDE = 16
CMP_HIDDEN = 128
SEL_BLOCK = 64
SEL_TOPK = 16
WINDOW = 512
D_FF = 2816
EPS = 1e-6
NEG_INF = -1e30
FORCE_SCORE = 1e6
SCALE = HEAD_DIM ** -0.5

LANES = 128
MAX_SEL_BLOCKS = 64
SEL_MASK_BIAS = -(2.0 ** 100)
VMEM_LIMIT = 56 * 1024 * 1024

ROW_TILE = 512
Q_TILE = 256
HALO = 16
FF_CHUNK = 1024

F32 = jnp.float32
BF16 = jnp.bfloat16
NT_DIMS = (((1,), (1,)), ((), ()))


def _rms(x, g):
    return x * lax.rsqrt(jnp.mean(x * x, axis=-1, keepdims=True) + EPS) * g


def _dot(a, b):
    return jnp.dot(a, b, preferred_element_type=F32)


def _dot_nt(a, b):
    return lax.dot_general(a, b, NT_DIMS, preferred_element_type=F32)


def _inproj_kernel(x_ref, n1_ref, wu_ref, wq_ref, wkv_ref, wg_ref, wpool_ref, spool_ref, npool_ref,
                   pool_ref, q_ref, kc_ref, vc_ref, gate_ref,
                   ksa_ref, ksb_ref, vsa_ref, vsb_ref, kwa_ref, kwb_ref, vwa_ref, vwb_ref,
                   ext_ref):
    t = pl.program_id(1)
    tt = x_ref.shape[1]
    h = _rms(x_ref[0], n1_ref[...]).astype(BF16)

    q_ref[0] = _dot(h, wq_ref[...]).astype(BF16)
    gate_ref[0] = jax.nn.sigmoid(_dot(h, wg_ref[...]))

    kv = _dot(h, wkv_ref[...])
    kc_ref[0] = kv[:, 0:KV_WIDTH]
    vc_ref[0] = kv[:, KV_WIDTH:2 * KV_WIDTH]

    lane = lax.broadcasted_iota(jnp.int32, (tt, LANES), 1)
    row = lax.broadcasted_iota(jnp.int32, (tt, LANES), 0)
    lo = lane < HEAD_DIM
    blk = (t * tt + row) // SEL_BLOCK
    onehot_hi = jnp.where(lane - HEAD_DIM == blk, 1.0, 0.0)
    onehot_lo = jnp.where(lane == blk, 1.0, 0.0)

    def emit(pair, a_ref, b_ref, fill_hi, fill_lo):
        rolled = pltpu.roll(pair, HEAD_DIM, axis=1)
        a_ref[0, 0] = jnp.where(lo, pair, fill_hi).astype(BF16)
        b_ref[0, 0] = jnp.where(lo, fill_lo, rolled).astype(BF16)
        a_ref[0, 1] = jnp.where(lo, rolled, fill_hi).astype(BF16)
        b_ref[0, 1] = jnp.where(lo, fill_lo, pair).astype(BF16)

    emit(kv[:, 2 * KV_WIDTH:3 * KV_WIDTH], ksa_ref, ksb_ref, onehot_hi, onehot_lo)
    emit(kv[:, 3 * KV_WIDTH:4 * KV_WIDTH], vsa_ref, vsb_ref, 1.0, 1.0)
    emit(kv[:, 4 * KV_WIDTH:5 * KV_WIDTH], kwa_ref, kwb_ref, 0.0, 0.0)
    emit(kv[:, 5 * KV_WIDTH:6 * KV_WIDTH], vwa_ref, vwb_ref, 1.0, 1.0)

    u = _dot(h, wu_ref[...])

    @pl.when(t == 0)
    def _():
        ext_ref[0:HALO, :] = jnp.zeros((HALO, POOL_WIDTH), F32)

    @pl.when(t > 0)
    def _():
        ext_ref[0:HALO, :] = ext_ref[tt:tt + HALO, :]

    ext_ref[HALO:HALO + tt, :] = u
    pos = t * tt + lax.broadcasted_iota(jnp.int32, (tt, 1), 0)
    mixed = []
    for p, w in enumerate(POOL_WINDOWS):
        cols = slice(p * POOL_GROUP, (p + 1) * POOL_GROUP)
        acc = u[:, cols]
        for k in range(1, w):
            acc = acc + ext_ref[HALO - k:HALO - k + tt, cols]
        cnt = jnp.minimum(pos + 1, w).astype(F32)
        pooled = acc / cnt - u[:, cols]
        mixed.append(_dot(pooled.astype(BF16), wpool_ref[p]))
    mixed = jnp.concatenate(mixed, axis=1) * spool_ref[...]
    pool_ref[0] = _rms(mixed, npool_ref[...])


def _inproj(x, n1, wu, wq, wkv, wg, wpool, spool, npool):
    B, T, D = x.shape
    tt = min(ROW_TILE, T)
    row3 = lambda w: pl.BlockSpec((1, tt, w), lambda b, t: (b, t, 0))
    grp = pl.BlockSpec((1, N_KV_GROUPS, tt, LANES), lambda b, t: (b, 0, t, 0))
    full = lambda a: pl.BlockSpec(a.shape, lambda b, t: (0,) * a.ndim)
    kv_shape = jax.ShapeDtypeStruct((B, N_KV_GROUPS, T, LANES), BF16)
    return pl.pallas_call(
        _inproj_kernel,
        grid=(B, T // tt),
        in_specs=[row3(D), full(n1), full(wu), full(wq), full(wkv), full(wg), full(wpool), full(spool),
                  full(npool)],
        out_specs=[row3(POOL_WIDTH), row3(ATTN_WIDTH), row3(KV_WIDTH), row3(KV_WIDTH),
                   row3(N_KV_GROUPS * LANES)] + [grp] * 8,
        out_shape=[jax.ShapeDtypeStruct((B, T, POOL_WIDTH), F32),
                   jax.ShapeDtypeStruct((B, T, ATTN_WIDTH), BF16),
                   jax.ShapeDtypeStruct((B, T, KV_WIDTH), F32),
                   jax.ShapeDtypeStruct((B, T, KV_WIDTH), F32),
                   jax.ShapeDtypeStruct((B, T, N_KV_GROUPS * LANES), F32)] + [kv_shape] * 8,
        scratch_shapes=[pltpu.VMEM((HALO + tt, POOL_WIDTH), F32)],
        compiler_params=pltpu.CompilerParams(dimension_semantics=("arbitrary", "arbitrary"),
                                             vmem_limit_bytes=VMEM_LIMIT),
        name="inproj_pool",
    )(x, n1, wu, wq, wkv, wg, wpool, spool, npool)


def _compress_kernel(kc_ref, vc_ref, pek_ref, pev_ref, w1k_ref, w1v_ref, w2k_ref, w2v_ref,
                     kcmp_ref, vcmpt_ref):
    nc = kc_ref.shape[1]

    def hidden(c_ref, pe_ref, w1_ref):
        c = c_ref[0]
        top = _dot((c + pe_ref[0:1, :]).astype(BF16), w1_ref[0])
        bot = _dot((c + pe_ref[1:2, :]).astype(BF16), w1_ref[1])
        return jax.nn.gelu(top + pltpu.roll(bot, nc - 1, axis=0)).astype(BF16)

    kk = _dot(hidden(kc_ref, pek_ref, w1k_ref), w2k_ref[...])
    for i in range(2 * N_KV_GROUPS):
        kcmp_ref[0, i] = kk[:, i * LANES:(i + 1) * LANES].astype(BF16)
    vv = _dot(hidden(vc_ref, pev_ref, w1v_ref), w2v_ref[...])
    vcmpt_ref[0] = vv.T.astype(BF16)


def _compress(kc2, vc2, pek, pev, w1k, w1v, w2k, w2v):
    B, nc, width = kc2.shape
    full = lambda a: pl.BlockSpec(a.shape, lambda b: (0,) * a.ndim)
    chunk = pl.BlockSpec((1, nc, width), lambda b: (b, 0, 0))
    return pl.pallas_call(
        _compress_kernel,
        grid=(B,),
        in_specs=[chunk, chunk, full(pek), full(pev), full(w1k), full(w1v), full(w2k), full(w2v)],
        out_specs=[pl.BlockSpec((1, 2 * N_KV_GROUPS, nc, LANES), lambda b: (b, 0, 0, 0)),
                   pl.BlockSpec((1, KV_WIDTH, nc), lambda b: (b, 0, 0))],
        out_shape=[jax.ShapeDtypeStruct((B, 2 * N_KV_GROUPS, nc, LANES), BF16),
                   jax.ShapeDtypeStruct((B, KV_WIDTH, nc), BF16)],
        compiler_params=pltpu.CompilerParams(dimension_semantics=("arbitrary",),
                                             vmem_limit_bytes=VMEM_LIMIT),
        name="compress",
    )(kc2, vc2, pek, pev, w1k, w1v, w2k, w2v)


def _attn_kernel(q_ref, gate_ref, kc_ref, vct_ref, ovl_ref,
                 ksa_ref, ksb_ref, vsa_ref, vsb_ref, kwa_ref, kwb_ref, vwa_ref, vwb_ref,
                 o_ref, acc_ref, m_ref, *, topk):
    qt = pl.program_id(2)
    tq = q_ref.shape[1]
    nc = kc_ref.shape[2]
    seq = ksa_ref.shape[2]
    t0 = qt * tq
    t_lane = t0 + lax.broadcasted_iota(jnp.int32, (1, tq), 1)

    n_sub = lax.broadcasted_iota(jnp.int32, (nc, 1), 0)
    cmask = (n_sub * CMP_STRIDE + (CMP_LEN - 1)) <= t_lane
    psum = jnp.zeros((nc, tq), F32)
    o_cmp = []
    for r in range(HEADS_PER_GROUP):
        qpair = q_ref[0, :, (r // 2) * LANES:(r // 2 + 1) * LANES]
        s = _dot_nt(kc_ref[0, r % 2], qpair) * SCALE
        s = jnp.where(cmask, s, NEG_INF)
        e = jnp.where(cmask, jnp.exp(s - jnp.max(s, axis=0, keepdims=True)), 0.0)
        l = jnp.sum(e, axis=0, keepdims=True)
        p = e / jnp.where(l > 0.0, l, 1.0)
        psum = psum + p
        o_cmp.append(_dot(vct_ref[0], p.astype(BF16)))

    imp = jnp.dot(ovl_ref[...], psum, precision=lax.Precision.HIGHEST, preferred_element_type=F32)
    j_sub = lax.broadcasted_iota(jnp.int32, (MAX_SEL_BLOCKS, 1), 0)
    cur = t_lane // SEL_BLOCK
    forced = (j_sub == 0) | (j_sub == cur) | (j_sub == cur - 1)
    valid = j_sub * SEL_BLOCK <= t_lane
    imp = jnp.where(forced, FORCE_SCORE, imp)
    imp = jnp.where(valid, imp, NEG_INF)
    rank = jnp.zeros((MAX_SEL_BLOCKS, tq), F32)
    for jp in range(MAX_SEL_BLOCKS):
        other = imp[jp:jp + 1, :]
        tie = jnp.where(j_sub > jp, 1.0, 0.0)
        rank = rank + jnp.where(other > imp, 1.0, jnp.where(other == imp, tie, 0.0))
    bias_t = jnp.where(valid, jnp.where(rank < topk, 0.0, SEL_MASK_BIAS), SEL_MASK_BIAS)
    bias = jnp.concatenate([bias_t, bias_t], axis=0).T

    lane = lax.broadcasted_iota(jnp.int32, (tq, LANES), 1)
    lo = lane < HEAD_DIM
    qa, qb = [], []
    for k in range(HEADS_PER_GROUP // 2):
        qs = q_ref[0, :, k * LANES:(k + 1) * LANES].astype(F32) * SCALE
        qa.append(jnp.where(lo, qs, bias).astype(BF16))
        qb.append(jnp.where(lo, bias, qs).astype(BF16))
    q_forms = (jnp.concatenate(qa, axis=0), jnp.concatenate(qb, axis=0))
    t_row = t0 + lax.broadcasted_iota(jnp.int32, (2 * tq, 1), 0) % tq

    def normalize(acc):
        return acc / pltpu.roll(acc, HEAD_DIM, axis=1)

    def selected(form, k_ref, v_ref):
        q_aug = q_forms[form]
        m_ref[form] = jnp.full((2 * tq, 1), NEG_INF, F32)
        acc_ref[form] = jnp.zeros((2 * tq, LANES), F32)

        def body(kt, carry):
            start = pl.multiple_of(kt * tq, tq)
            s = _dot_nt(q_aug, k_ref[0, 0, pl.ds(start, tq), :])
            kpos = start + lax.broadcasted_iota(jnp.int32, (1, tq), 1)
            s = jnp.where(kpos <= t_row, s, NEG_INF)
            m_old = m_ref[form]
            m_new = jnp.maximum(m_old, jnp.max(s, axis=1, keepdims=True))
            p = jnp.exp(s - m_new)
            acc_ref[form] = (jnp.exp(m_old - m_new) * acc_ref[form]
                             + _dot(p.astype(BF16), v_ref[0, 0, pl.ds(start, tq), :]))
            m_ref[form] = m_new
            return carry

        lax.fori_loop(0, qt + 1, body, 0)
        return normalize(acc_ref[form])

    def window(form, k_ref, v_ref):
        span = WINDOW + tq
        start = pl.multiple_of(jnp.maximum(t0 - WINDOW, 0), tq)
        s = _dot_nt(q_forms[form], k_ref[0, 0, pl.ds(start, span), :])
        kpos = start + lax.broadcasted_iota(jnp.int32, (1, span), 1)
        s = jnp.where((kpos <= t_row) & (kpos > t_row - WINDOW), s, NEG_INF)
        p = jnp.exp(s - jnp.max(s, axis=1, keepdims=True))
        return normalize(_dot(p.astype(BF16), v_ref[0, 0, pl.ds(start, span), :]))

    o_sel = (selected(0, ksa_ref, vsa_ref), selected(1, ksb_ref, vsb_ref))
    o_win = (window(0, kwa_ref, vwa_ref), window(1, kwb_ref, vwb_ref))

    gates = gate_ref[0]
    for k in range(HEADS_PER_GROUP // 2):
        rows = slice(k * tq, (k + 1) * tq)
        branches = (jnp.concatenate([o_cmp[2 * k], o_cmp[2 * k + 1]], axis=0).T,
                    jnp.where(lo, o_sel[0][rows], o_sel[1][rows]),
                    jnp.where(lo, o_win[0][rows], o_win[1][rows]))
        out = jnp.zeros((tq, LANES), F32)
        for br in range(N_BRANCHES):
            c0 = (2 * k) * N_BRANCHES + br
            c1 = (2 * k + 1) * N_BRANCHES + br
            g = jnp.where(lo, jnp.broadcast_to(gates[:, c0:c0 + 1], (tq, LANES)),
                          jnp.broadcast_to(gates[:, c1:c1 + 1], (tq, LANES)))
            out = out + g * branches[br]
        o_ref[0, :, k * LANES:(k + 1) * LANES] = out


def _attention(q, gates, kcmp, vcmpt, ovl, kv_forms):
    B, T, _ = q.shape
    nc = kcmp.shape[2]
    tq = min(Q_TILE, T)
    assert T % tq == 0 and T >= WINDOW + tq and T // SEL_BLOCK <= MAX_SEL_BLOCKS
    topk = min(SEL_TOPK, T // SEL_BLOCK)
    kv_spec = pl.BlockSpec((1, 1, T, LANES), lambda b, g, t: (b, g, 0, 0))
    return pl.pallas_call(
        functools.partial(_attn_kernel, topk=topk),
        grid=(B, N_KV_GROUPS, T // tq),
        in_specs=[pl.BlockSpec((1, tq, GROUP_WIDTH), lambda b, g, t: (b, t, g)),
                  pl.BlockSpec((1, tq, LANES), lambda b, g, t: (b, t, g)),
                  pl.BlockSpec((1, 2, nc, LANES), lambda b, g, t: (b, g, 0, 0)),
                  pl.BlockSpec((1, HEAD_DIM, nc), lambda b, g, t: (b, g, 0)),
                  pl.BlockSpec(ovl.shape, lambda b, g, t: (0, 0))] + [kv_spec] * 8,
        out_specs=pl.BlockSpec((1, tq, GROUP_WIDTH), lambda b, g, t: (b, t, g)),
        out_shape=jax.ShapeDtypeStruct((B, T, ATTN_WIDTH), F32),
        scratch_shapes=[pltpu.VMEM((2, 2 * tq, LANES), F32), pltpu.VMEM((2, 2 * tq, 1), F32)],
        compiler_params=pltpu.CompilerParams(dimension_semantics=("arbitrary",) * 3,
                                             vmem_limit_bytes=VMEM_LIMIT),
        name="nsa_attention",
    )(q, gates, kcmp, vcmpt, ovl, *kv_forms)


def _ffn_kernel(x_ref, pool_ref, attn_ref, nattn_ref, wo_ref, n2_ref, wg_ref, wu_ref, cw_ref, cb_ref,
                wd_ref, nf_ref, o_ref, prev_ref, *, final):
    t = pl.program_id(1)
    tt = x_ref.shape[1]
    attn = _rms(attn_ref[0], nattn_ref[...]).astype(BF16)
    x1 = (x_ref[0] + _dot(pool_ref[0].astype(BF16), wo_ref[0:POOL_WIDTH, :])
          + _dot(attn, wo_ref[POOL_WIDTH:, :]))
    h = _rms(x1, n2_ref[...]).astype(BF16)

    @pl.when(t == 0)
    def _():
        prev_ref[...] = jnp.zeros(prev_ref.shape, F32)

    row = lax.broadcasted_iota(jnp.int32, (tt, 1), 0)
    o_ref[0] = x1
    for c0 in range(0, D_FF, FF_CHUNK):
        cols = slice(c0, min(c0 + FF_CHUNK, D_FF))
        g = _dot(h, wg_ref[:, cols])
        p1 = prev_ref[7:8, cols]
        p2 = prev_ref[6:7, cols]
        g1 = jnp.where(row == 0, p1, pltpu.roll(g, 1, axis=0))
        g2 = jnp.where(row == 0, p2, jnp.where(row == 1, p1, pltpu.roll(g, 2, axis=0)))
        prev_ref[:, cols] = g[tt - 8:tt, :]
        gc = (g * cw_ref[2:3, cols] + g1 * cw_ref[1:2, cols] + g2 * cw_ref[0:1, cols]
              + cb_ref[:, cols])
        act = (jax.nn.silu(gc) * _dot(h, wu_ref[:, cols])).astype(BF16)
        o_ref[0] += _dot(act, wd_ref[cols, :])
    if final:
        o_ref[0] = _rms(o_ref[0], nf_ref[...])


def _ffn(x, pool, attn, nattn, wo, n2, wg, wu, cw, cb, wd, nf, final):
    B, T, D = x.shape
    tt = min(ROW_TILE, T)
    row3 = lambda w: pl.BlockSpec((1, tt, w), lambda b, t: (b, t, 0))
    full = lambda a: pl.BlockSpec(a.shape, lambda b, t: (0,) * a.ndim, pipeline_mode=pl.Buffered(1))
    return pl.pallas_call(
        functools.partial(_ffn_kernel, final=final),
        grid=(B, T // tt),
        in_specs=[row3(D), row3(POOL_WIDTH), row3(ATTN_WIDTH), full(nattn), full(wo), full(n2), full(wg),
                  full(wu), full(cw), full(cb), full(wd), full(nf)],
        out_specs=row3(D),
        out_shape=jax.ShapeDtypeStruct((B, T, D), F32),
        scratch_shapes=[pltpu.VMEM((8, D_FF), F32)],
        compiler_params=pltpu.CompilerParams(dimension_semantics=("arbitrary", "arbitrary"),
                                             vmem_limit_bytes=VMEM_LIMIT),
        name="outproj_ffn",
    )(x, pool, attn, nattn, wo, n2, wg, wu, cw, cb, wd, nf)


def _compress_weights(pe, w1, w2, k_forms):
    half = CMP_LEN // 2
    eye = jnp.eye(N_KV_GROUPS, dtype=F32)
    pe_flat = jnp.broadcast_to(pe.reshape(2, half, 1, HEAD_DIM), (2, half, N_KV_GROUPS, HEAD_DIM))
    pe_flat = pe_flat.reshape(2, half * KV_WIDTH)
    w1r = w1.reshape(2, half, HEAD_DIM, CMP_HIDDEN)
    w1s = jnp.einsum("hldc,ge->hlgdec", w1r, eye).reshape(2, half * KV_WIDTH, N_KV_GROUPS * CMP_HIDDEN)
    if k_forms:
        place = jnp.zeros((2, HEAD_DIM, LANES), F32)
        place = place.at[0, :, :HEAD_DIM].set(jnp.eye(HEAD_DIM)).at[1, :, HEAD_DIM:].set(jnp.eye(HEAD_DIM))
        w2s = jnp.einsum("cd,ge,fdm->gcefm", w2, eye, place)
        w2s = w2s.reshape(N_KV_GROUPS * CMP_HIDDEN, N_KV_GROUPS * 2 * LANES)
    else:
        w2s = jnp.einsum("cd,ge->gced", w2, eye).reshape(N_KV_GROUPS * CMP_HIDDEN, KV_WIDTH)
    return pe_flat, w1s.astype(BF16), w2s.astype(BF16)


def _overlap_t(n_cmp_pad, n_cmp, n_sel):
    cmp_start = np.arange(n_cmp_pad) * CMP_STRIDE
    cmp_end = cmp_start + CMP_LEN - 1
    blk_start = np.arange(MAX_SEL_BLOCKS) * SEL_BLOCK
    ovl = (cmp_start[None, :] < blk_start[:, None] + SEL_BLOCK) & (cmp_end[None, :] >= blk_start[:, None])
    ovl &= (np.arange(n_cmp_pad)[None, :] < n_cmp) & (np.arange(MAX_SEL_BLOCKS)[:, None] < n_sel)
    return jnp.asarray(ovl.astype(np.float32))


def kernel(x, norm1, w_in, w_pool, s_pool, cmp_pe_k, cmp_w1_k, cmp_w2_k, cmp_pe_v, cmp_w1_v, cmp_w2_v,
           norm_pool_out, norm_attn_out, w_out, norm2, w_gate, w_up, conv_w, conv_b, w_down, norm_f):
    B, T, D = x.shape
    depth = w_in.shape[0]
    nc = T // CMP_STRIDE
    n_cmp = (T - CMP_LEN) // CMP_STRIDE + 1
    ovl = _overlap_t(nc, n_cmp, T // SEL_BLOCK)
    row = lambda v: v.reshape(1, -1)
    n_gate = HEADS_PER_GROUP * N_BRANCHES
    for l in range(depth):
        w = w_in[l].astype(BF16)
        o_kv = POOL_WIDTH + ATTN_WIDTH
        o_g = o_kv + 6 * KV_WIDTH
        wg = jnp.zeros((D, N_KV_GROUPS * LANES), BF16)
        for g in range(N_KV_GROUPS):
            wg = wg.at[:, g * LANES:g * LANES + n_gate].set(w[:, o_g + g * n_gate:o_g + (g + 1) * n_gate])
        pool, q, kc, vc, gates, *kv_forms = _inproj(
            x, row(norm1[l]), w[:, :POOL_WIDTH], w[:, POOL_WIDTH:o_kv], w[:, o_kv:o_g], wg,
            w_pool[l].astype(BF16), row(s_pool[l]), row(norm_pool_out[l]))
        pek, w1k, w2k = _compress_weights(cmp_pe_k[l], cmp_w1_k[l], cmp_w2_k[l], True)
        pev, w1v, w2v = _compress_weights(cmp_pe_v[l], cmp_w1_v[l], cmp_w2_v[l], False)
        kcmp, vcmpt = _compress(kc.reshape(B, nc, CMP_STRIDE * KV_WIDTH), vc.reshape(B, nc, CMP_STRIDE * KV_WIDTH),
                                pek, pev, w1k, w1v, w2k, w2v)
        attn = _attention(q, gates, kcmp, vcmpt, ovl, kv_forms)
        x = _ffn(x, pool, attn, row(norm_attn_out[l]), w_out[l].astype(BF16), row(norm2[l]),
                 w_gate[l].astype(BF16), w_up[l].astype(BF16), conv_w[l], row(conv_b[l]),
                 w_down[l].astype(BF16), row(norm_f), final=(l == depth - 1))
    return x
```

```python
import functools

import jax
import jax.numpy as jnp
from jax import lax
from jax.experimental import pallas as pl
from jax.experimental.pallas import tpu as pltpu

D_MODEL = 1024
POOL_WIDTH = 512
POOL_WINDOWS = (2, 4, 8, 16)
POOL_GROUP = 128
ATTN_WIDTH = 512
HEAD_DIM = 64
N_KV_GROUPS = 2
HEADS_PER_GROUP = 4
GROUP_WIDTH = HEADS_PER_GROUP * HEAD_DIM
KV_WIDTH = N_KV_GROUPS * HEAD_DIM
N_BRANCHES = 3
CMP_LEN = 32
CMP_STRIDE = 16
CMP_HIDDEN = 128
CMP_PER_SEL = 4
SEL_BLOCK = 64
SEL_TOPK = 16
WINDOW = 512
D_FF = 2816
EPS = 1e-6
NEG_INF = -1e30
FORCE_SCORE = 1e6
SCALE = HEAD_DIM ** -0.5

LANES = 128
SUBLANES = 8
BF16_ROWS = 16
MAX_SEL_BLOCKS = 64
SEL_MASK_BIAS = -(2.0 ** 100)
VT_ROWS = HEAD_DIM + BF16_ROWS
VMEM_LIMIT = 56 * 1024 * 1024

ROW_TILE = 512
Q_TILE = 256
SEL_KEY_TILE = 1024
HALO = 16
FF_CHUNK = 1024

F32 = jnp.float32
BF16 = jnp.bfloat16
NT_DIMS = (((1,), (1,)), ((), ()))


def _rms(x, g):
    return x * lax.rsqrt(jnp.mean(x * x, axis=-1, keepdims=True) + EPS) * g


def _dot(a, b):
    return jnp.dot(a, b, preferred_element_type=F32)


def _dot_nt(a, b):
    return lax.dot_general(a, b, NT_DIMS, preferred_element_type=F32)


def _inproj_kernel(x_ref, n1_ref, wu_ref, wq_ref, wkv_ref, wg_ref, wpool_ref, spool_ref, npool_ref,
                   pool_ref, q_ref, kc_ref, vc_ref, gate_ref,
                   ksa_ref, ksb_ref, vst_ref, kwa_ref, kwb_ref, vwt_ref,
                   ext_ref):
    t = pl.program_id(1)
    tt = x_ref.shape[1]
    h = _rms(x_ref[0], n1_ref[...]).astype(BF16)

    q_ref[0] = (_dot(h, wq_ref[...]) * SCALE).astype(BF16)
    gate_ref[0] = jax.nn.sigmoid(_dot(h, wg_ref[...]))

    kv = _dot(h, wkv_ref[...])
    kc_ref[0] = kv[:, 0:KV_WIDTH]
    vc_ref[0] = kv[:, KV_WIDTH:2 * KV_WIDTH]

    lane = lax.broadcasted_iota(jnp.int32, (tt, LANES), 1)
    row = lax.broadcasted_iota(jnp.int32, (tt, LANES), 0)
    lo = lane < HEAD_DIM
    blk = (t * tt + row) // SEL_BLOCK
    onehot_hi = jnp.where(lane - HEAD_DIM == blk, 1.0, 0.0)
    onehot_lo = jnp.where(lane == blk, 1.0, 0.0)

    def emit_k(pair, a_ref, b_ref, fill_hi, fill_lo):
        rolled = pltpu.roll(pair, HEAD_DIM, axis=1)
        a_ref[0, 0] = jnp.where(lo, pair, fill_hi).astype(BF16)
        b_ref[0, 0] = jnp.where(lo, fill_lo, rolled).astype(BF16)
        a_ref[0, 1] = jnp.where(lo, rolled, fill_hi).astype(BF16)
        b_ref[0, 1] = jnp.where(lo, fill_lo, pair).astype(BF16)

    def emit_vt(pair, vt_ref):
        pt = pair.T.astype(BF16)
        for g in range(N_KV_GROUPS):
            vt_ref[0, g, 0:HEAD_DIM, :] = pt[g * HEAD_DIM:(g + 1) * HEAD_DIM]
            vt_ref[0, g, HEAD_DIM:VT_ROWS, :] = jnp.ones((BF16_ROWS, tt), BF16)

    emit_k(kv[:, 2 * KV_WIDTH:3 * KV_WIDTH], ksa_ref, ksb_ref, onehot_hi, onehot_lo)
    emit_vt(kv[:, 3 * KV_WIDTH:4 * KV_WIDTH], vst_ref)
    emit_k(kv[:, 4 * KV_WIDTH:5 * KV_WIDTH], kwa_ref, kwb_ref, 0.0, 0.0)
    emit_vt(kv[:, 5 * KV_WIDTH:6 * KV_WIDTH], vwt_ref)

    u = _dot(h, wu_ref[...])

    @pl.when(t == 0)
    def _():
        ext_ref[0:HALO, :] = jnp.zeros((HALO, POOL_WIDTH), F32)

    @pl.when(t > 0)
    def _():
        ext_ref[0:HALO, :] = ext_ref[tt:tt + HALO, :]

    ext_ref[HALO:HALO + tt, :] = u
    pos = t * tt + lax.broadcasted_iota(jnp.int32, (tt, 1), 0)
    mixed = []
    for p, w in enumerate(POOL_WINDOWS):
        cols = slice(p * POOL_GROUP, (p + 1) * POOL_GROUP)
        acc = u[:, cols]
        for k in range(1, w):
            acc = acc + ext_ref[HALO - k:HALO - k + tt, cols]
        cnt = jnp.minimum(pos + 1, w).astype(F32)
        pooled = acc / cnt - u[:, cols]
        mixed.append(_dot(pooled.astype(BF16), wpool_ref[p]))
    mixed = jnp.concatenate(mixed, axis=1) * spool_ref[...]
    pool_ref[0] = _rms(mixed, npool_ref[...])


def _inproj(x, n1, wu, wq, wkv, wg, wpool, spool, npool):
    B, T, D = x.shape
    tt = min(ROW_TILE, T)
    row3 = lambda w: pl.BlockSpec((1, tt, w), lambda b, t: (b, t, 0))
    k_spec = pl.BlockSpec((1, N_KV_GROUPS, tt, LANES), lambda b, t: (b, 0, t, 0))
    vt_spec = pl.BlockSpec((1, N_KV_GROUPS, VT_ROWS, tt), lambda b, t: (b, 0, 0, t))
    full = lambda a: pl.BlockSpec(a.shape, lambda b, t: (0,) * a.ndim)
    k_shape = jax.ShapeDtypeStruct((B, N_KV_GROUPS, T, LANES), BF16)
    vt_shape = jax.ShapeDtypeStruct((B, N_KV_GROUPS, VT_ROWS, T), BF16)
    return pl.pallas_call(
        _inproj_kernel,
        grid=(B, T // tt),
        in_specs=[row3(D), full(n1), full(wu), full(wq), full(wkv), full(wg), full(wpool), full(spool),
                  full(npool)],
        out_specs=[row3(POOL_WIDTH), row3(ATTN_WIDTH), row3(KV_WIDTH), row3(KV_WIDTH),
                   row3(N_KV_GROUPS * LANES), k_spec, k_spec, vt_spec, k_spec, k_spec, vt_spec],
        out_shape=[jax.ShapeDtypeStruct((B, T, POOL_WIDTH), F32),
                   jax.ShapeDtypeStruct((B, T, ATTN_WIDTH), BF16),
                   jax.ShapeDtypeStruct((B, T, KV_WIDTH), F32),
                   jax.ShapeDtypeStruct((B, T, KV_WIDTH), F32),
                   jax.ShapeDtypeStruct((B, T, N_KV_GROUPS * LANES), F32),
                   k_shape, k_shape, vt_shape, k_shape, k_shape, vt_shape],
        scratch_shapes=[pltpu.VMEM((HALO + tt, POOL_WIDTH), F32)],
        compiler_params=pltpu.CompilerParams(dimension_semantics=("arbitrary", "arbitrary"),
                                             vmem_limit_bytes=VMEM_LIMIT),
        name="inproj_pool",
    )(x, n1, wu, wq, wkv, wg, wpool, spool, npool)


def _compress_kernel(kc_ref, vc_ref, pek_ref, pev_ref, w1k_ref, w1v_ref, w2k_ref, w2v_ref,
                     kcmp_ref, vcmpt_ref):
    nc = kc_ref.shape[1]

    def hidden(c_ref, pe_ref, w1_ref):
        c = c_ref[0]
        top = _dot((c + pe_ref[0:1, :]).astype(BF16), w1_ref[0])
        bot = _dot((c + pe_ref[1:2, :]).astype(BF16), w1_ref[1])
        return jax.nn.gelu(top + pltpu.roll(bot, nc - 1, axis=0)).astype(BF16)

    kk = _dot(hidden(kc_ref, pek_ref, w1k_ref), w2k_ref[...])
    for i in range(2 * N_KV_GROUPS):
        kcmp_ref[0, i] = kk[:, i * LANES:(i + 1) * LANES].astype(BF16)
    vv = _dot(hidden(vc_ref, pev_ref, w1v_ref), w2v_ref[...])
    vcmpt_ref[0] = vv.T.astype(BF16)


def _compress(kc2, vc2, pek, pev, w1k, w1v, w2k, w2v):
    B, nc, width = kc2.shape
    full = lambda a: pl.BlockSpec(a.shape, lambda b: (0,) * a.ndim)
    chunk = pl.BlockSpec((1, nc, width), lambda b: (b, 0, 0))
    return pl.pallas_call(
        _compress_kernel,
        grid=(B,),
        in_specs=[chunk, chunk, full(pek), full(pev), full(w1k), full(w1v), full(w2k), full(w2v)],
        out_specs=[pl.BlockSpec((1, 2 * N_KV_GROUPS, nc, LANES), lambda b: (b, 0, 0, 0)),
                   pl.BlockSpec((1, KV_WIDTH, nc), lambda b: (b, 0, 0))],
        out_shape=[jax.ShapeDtypeStruct((B, 2 * N_KV_GROUPS, nc, LANES), BF16),
                   jax.ShapeDtypeStruct((B, KV_WIDTH, nc), BF16)],
        compiler_params=pltpu.CompilerParams(dimension_semantics=("arbitrary",),
                                             vmem_limit_bytes=VMEM_LIMIT),
        name="compress",
    )(kc2, vc2, pek, pev, w1k, w1v, w2k, w2v)


def _attn_kernel(q_ref, gate_ref, kc_ref, vct_ref, ksa_ref, ksb_ref, vst_ref, kwa_ref, kwb_ref, vwt_ref,
                 o_ref, psum_ref, bias_ref, acc_ref, m_ref, *, topk):
    qt = pl.program_id(2)
    tq = q_ref.shape[1]
    nc = kc_ref.shape[2]
    ns = nc // CMP_PER_SEL
    t0 = qt * tq
    t_lane = t0 + lax.broadcasted_iota(jnp.int32, (1, tq), 1)

    def normalized(acc):
        return acc[0:HEAD_DIM] / acc[HEAD_DIM:HEAD_DIM + 1]

    q_pairs = jnp.concatenate([q_ref[0, :, k * LANES:(k + 1) * LANES] for k in range(HEADS_PER_GROUP // 2)],
                              axis=0)
    span = WINDOW + tq
    start = pl.multiple_of(jnp.maximum(t0 - WINDOW, 0), tq)
    s_win = [_dot_nt(k_ref[0, 0, pl.ds(start, span), :], q_pairs) for k_ref in (kwa_ref, kwb_ref)]
    kpos = start + lax.broadcasted_iota(jnp.int32, (span, 1), 0)
    wbias = jnp.where(kpos <= t_lane, jnp.where(kpos > t_lane - WINDOW, 0.0, NEG_INF), NEG_INF)
    wbias2 = jnp.concatenate([wbias, wbias], axis=1)
    o_win = []
    for f in range(2):
        s = s_win[f] + wbias2
        p = jnp.exp(s - jnp.max(s, axis=0, keepdims=True))
        o_win.append(normalized(_dot(vwt_ref[0, 0, :, pl.ds(start, span)], p.astype(BF16))))

    n_sub = lax.broadcasted_iota(jnp.int32, (nc, 1), 0)
    cbias = jnp.where(n_sub * CMP_STRIDE + (CMP_LEN - 1) <= t_lane, 0.0, NEG_INF)
    any_visible = jnp.where(t_lane >= CMP_LEN - 1, 1.0, 0.0)
    cbias2 = jnp.concatenate([cbias, cbias], axis=1)
    visible2 = jnp.concatenate([any_visible, any_visible], axis=1)
    psum = jnp.zeros((nc, tq), F32)
    o_cmp = [None] * HEADS_PER_GROUP
    s_cmp = [_dot_nt(kc_ref[0, f], q_pairs) for f in range(2)]
    for f in range(2):
        s = s_cmp[f] + cbias2
        e = jnp.exp(s - jnp.max(s, axis=0, keepdims=True))
        p = e * (visible2 / jnp.sum(e, axis=0, keepdims=True))
        psum = psum + p[:, 0:tq] + p[:, tq:2 * tq]
        o = _dot(vct_ref[0], p.astype(BF16))
        o_cmp[f], o_cmp[f + 2] = o[:, 0:tq], o[:, tq:2 * tq]
    for i in range(tq // LANES):
        psum_ref[i] = psum[:, i * LANES:(i + 1) * LANES]

    j_sub = lax.broadcasted_iota(jnp.int32, (ns, 1), 0)
    parts = [jnp.concatenate([psum_ref[i, pl.ds(k, ns, stride=CMP_PER_SEL), :] for i in range(tq // LANES)],
                             axis=1) for k in range(CMP_PER_SEL)]
    before = jnp.where(j_sub == 0, 0.0, pltpu.roll(parts[CMP_PER_SEL - 1], 1, axis=0))
    imp = parts[0] + parts[1] + parts[2] + parts[3] + before
    cur = t_lane // SEL_BLOCK
    forced = (j_sub == 0) | (j_sub == cur) | (j_sub == cur - 1)
    valid = j_sub * SEL_BLOCK <= t_lane
    imp = jnp.where(forced, FORCE_SCORE, imp)
    imp = jnp.where(valid, imp, NEG_INF)

    n_chunks = ns // SUBLANES
    chunks = [imp[c * SUBLANES:(c + 1) * SUBLANES] for c in range(n_chunks)]
    sub8 = lax.broadcasted_iota(jnp.int32, (SUBLANES, 1), 0)
    bias_ref[...] = jnp.full(bias_ref.shape, SEL_MASK_BIAS, F32)
    causal_chunks = (t0 + tq - 1) // (SEL_BLOCK * SUBLANES) + 1
    for nv in range(1, n_chunks + 1):
        @pl.when(causal_chunks == nv)
        def _():
            ranks = [jnp.zeros((SUBLANES, tq), F32) for _ in range(nv)]
            for jp in range(nv * SUBLANES):
                cj, rj = divmod(jp, SUBLANES)
                other = chunks[cj][rj:rj + 1, :]
                for c in range(nv):
                    if c > cj:
                        before_me = jnp.where(other >= chunks[c], 1.0, 0.0)
                    elif c < cj:
                        before_me = jnp.where(other > chunks[c], 1.0, 0.0)
                    else:
                        before_me = jnp.where(sub8 > rj, jnp.where(other >= chunks[c], 1.0, 0.0),
                                              jnp.where(other > chunks[c], 1.0, 0.0))
                    ranks[c] = ranks[c] + before_me
            for c in range(nv):
                rows = slice(c * SUBLANES, (c + 1) * SUBLANES)
                bias_ref[rows, :] = jnp.where(valid[rows], jnp.where(ranks[c] < topk, 0.0, SEL_MASK_BIAS),
                                              SEL_MASK_BIAS)
    bias_t = bias_ref[...]
    bias = jnp.concatenate([bias_t, bias_t], axis=0).T

    lane = lax.broadcasted_iota(jnp.int32, (tq, LANES), 1)
    lo = lane < HEAD_DIM
    qa, qb = [], []
    for k in range(HEADS_PER_GROUP // 2):
        qs = q_ref[0, :, k * LANES:(k + 1) * LANES].astype(F32)
        qa.append(jnp.where(lo, qs, bias).astype(BF16))
        qb.append(jnp.where(lo, bias, qs).astype(BF16))
    q_forms = (jnp.concatenate(qa, axis=0), jnp.concatenate(qb, axis=0))
    k_sel = (ksa_ref, ksb_ref)
    kt_sel = min(SEL_KEY_TILE, ksa_ref.shape[2])

    head = pl.multiple_of((t0 // kt_sel) * kt_sel, kt_sel)
    kpos_h = head + lax.broadcasted_iota(jnp.int32, (kt_sel, 1), 0)
    causal = jnp.where(kpos_h <= t_lane, 0.0, NEG_INF)
    causal2 = jnp.concatenate([causal, causal], axis=1)
    s_head = [_dot_nt(k_sel[f][0, 0, pl.ds(head, kt_sel), :], q_forms[f]) for f in range(2)]
    for f in range(2):
        s = s_head[f] + causal2
        m = jnp.max(s, axis=0, keepdims=True)
        p = jnp.exp(s - m)
        m_ref[f] = m
        acc_ref[f] = _dot(vst_ref[0, 0, :, pl.ds(head, kt_sel)], p.astype(BF16))

    def sel_tile(kt, carry):
        start = pl.multiple_of(kt * kt_sel, kt_sel)
        s_tile = [_dot_nt(k_sel[f][0, 0, pl.ds(start, kt_sel), :], q_forms[f]) for f in range(2)]
        for f in range(2):
            s = s_tile[f]
            m_old = m_ref[f]
            m_new = jnp.maximum(m_old, jnp.max(s, axis=0, keepdims=True))
            p = jnp.exp(s - m_new)
            acc_ref[f] = (jnp.exp(m_old - m_new) * acc_ref[f]
                          + _dot(vst_ref[0, 0, :, pl.ds(start, kt_sel)], p.astype(BF16)))
            m_ref[f] = m_new
        return carry

    lax.fori_loop(0, t0 // kt_sel, sel_tile, 0)
    o_sel = [normalized(acc_ref[f]) for f in range(2)]

    gates_t = gate_ref[0].T
    for k in range(HEADS_PER_GROUP // 2):
        heads = []
        for r in (2 * k, 2 * k + 1):
            cols = slice((r // 2) * tq, (r // 2 + 1) * tq)
            branches = (o_cmp[r], o_sel[r % 2][:, cols], o_win[r % 2][:, cols])
            out = jnp.zeros((HEAD_DIM, tq), F32)
            for br in range(N_BRANCHES):
                c = r * N_BRANCHES + br
                out = out + gates_t[c:c + 1, :] * branches[br]
            heads.append(out)
        o_ref[0, :, k * LANES:(k + 1) * LANES] = jnp.concatenate(heads, axis=0).T


def _attention(q, gates, kcmp, vcmpt, ksa, ksb, vst, kwa, kwb, vwt):
    B, T, _ = q.shape
    nc = kcmp.shape[2]
    tq = min(Q_TILE, T)
    assert T % tq == 0 and T >= WINDOW + tq and T // SEL_BLOCK <= MAX_SEL_BLOCKS
    topk = min(SEL_TOPK, T // SEL_BLOCK)
    k_spec = pl.BlockSpec((1, 1, T, LANES), lambda b, g, t: (b, g, 0, 0))
    vt_spec = pl.BlockSpec((1, 1, VT_ROWS, T), lambda b, g, t: (b, g, 0, 0))
    return pl.pallas_call(
        functools.partial(_attn_kernel, topk=topk),
        grid=(B, N_KV_GROUPS, T // tq),
        in_specs=[pl.BlockSpec((1, tq, GROUP_WIDTH), lambda b, g, t: (b, t, g)),
                  pl.BlockSpec((1, tq, LANES), lambda b, g, t: (b, t, g)),
                  pl.BlockSpec((1, 2, nc, LANES), lambda b, g, t: (b, g, 0, 0)),
                  pl.BlockSpec((1, HEAD_DIM, nc), lambda b, g, t: (b, g, 0)),
                  k_spec, k_spec, vt_spec, k_spec, k_spec, vt_spec],
        out_specs=pl.BlockSpec((1, tq, GROUP_WIDTH), lambda b, g, t: (b, t, g)),
        out_shape=jax.ShapeDtypeStruct((B, T, ATTN_WIDTH), F32),
        scratch_shapes=[pltpu.VMEM((tq // LANES, nc, LANES), F32),
                        pltpu.VMEM((MAX_SEL_BLOCKS, tq), F32),
                        pltpu.VMEM((2, VT_ROWS, 2 * tq), F32),
                        pltpu.VMEM((2, 1, 2 * tq), F32)],
        compiler_params=pltpu.CompilerParams(dimension_semantics=("arbitrary",) * 3,
                                             vmem_limit_bytes=VMEM_LIMIT),
        name="nsa_attention",
    )(q, gates, kcmp, vcmpt, ksa, ksb, vst, kwa, kwb, vwt)


def _ffn_kernel(x_ref, pool_ref, attn_ref, nattn_ref, wo_ref, n2_ref, wg_ref, wu_ref, cw_ref, cb_ref,
                wd_ref, nf_ref, o_ref, prev_ref, *, final):
    t = pl.program_id(1)
    tt = x_ref.shape[1]
    attn = _rms(attn_ref[0], nattn_ref[...]).astype(BF16)
    x1 = (x_ref[0] + _dot(pool_ref[0].astype(BF16), wo_ref[0:POOL_WIDTH, :])
          + _dot(attn, wo_ref[POOL_WIDTH:, :]))
    h = _rms(x1, n2_ref[...]).astype(BF16)

    @pl.when(t == 0)
    def _():
        prev_ref[...] = jnp.zeros(prev_ref.shape, F32)

    row = lax.broadcasted_iota(jnp.int32, (tt, 1), 0)
    o_ref[0] = x1
    for c0 in range(0, D_FF, FF_CHUNK):
        cols = slice(c0, min(c0 + FF_CHUNK, D_FF))
        g = _dot(h, wg_ref[:, cols])
        p1 = prev_ref[7:8, cols]
        p2 = prev_ref[6:7, cols]
        g1 = jnp.where(row == 0, p1, pltpu.roll(g, 1, axis=0))
        g2 = jnp.where(row == 0, p2, jnp.where(row == 1, p1, pltpu.roll(g, 2, axis=0)))
        prev_ref[:, cols] = g[tt - 8:tt, :]
        gc = (g * cw_ref[2:3, cols] + g1 * cw_ref[1:2, cols] + g2 * cw_ref[0:1, cols]
              + cb_ref[:, cols])
        act = (jax.nn.silu(gc) * _dot(h, wu_ref[:, cols])).astype(BF16)
        o_ref[0] += _dot(act, wd_ref[cols, :])
    if final:
        o_ref[0] = _rms(o_ref[0], nf_ref[...])


def _ffn(x, pool, attn, nattn, wo, n2, wg, wu, cw, cb, wd, nf, final):
    B, T, D = x.shape
    tt = min(ROW_TILE, T)
    row3 = lambda w: pl.BlockSpec((1, tt, w), lambda b, t: (b, t, 0))
    full = lambda a: pl.BlockSpec(a.shape, lambda b, t: (0,) * a.ndim, pipeline_mode=pl.Buffered(1))
    return pl.pallas_call(
        functools.partial(_ffn_kernel, final=final),
        grid=(B, T // tt),
        in_specs=[row3(D), row3(POOL_WIDTH), row3(ATTN_WIDTH), full(nattn), full(wo), full(n2), full(wg),
                  full(wu), full(cw), full(cb), full(wd), full(nf)],
        out_specs=row3(D),
        out_shape=jax.ShapeDtypeStruct((B, T, D), F32),
        scratch_shapes=[pltpu.VMEM((8, D_FF), F32)],
        compiler_params=pltpu.CompilerParams(dimension_semantics=("arbitrary", "arbitrary"),
                                             vmem_limit_bytes=VMEM_LIMIT),
        name="outproj_ffn",
    )(x, pool, attn, nattn, wo, n2, wg, wu, cw, cb, wd, nf)


def _compress_weights(pe, w1, w2, k_forms):
    half = CMP_LEN // 2
    eye = jnp.eye(N_KV_GROUPS, dtype=F32)
    pe_flat = jnp.broadcast_to(pe.reshape(2, half, 1, HEAD_DIM), (2, half, N_KV_GROUPS, HEAD_DIM))
    pe_flat = pe_flat.reshape(2, half * KV_WIDTH)
    w1r = w1.reshape(2, half, HEAD_DIM, CMP_HIDDEN)
    w1s = jnp.einsum("hldc,ge->hlgdec", w1r, eye).reshape(2, half * KV_WIDTH, N_KV_GROUPS * CMP_HIDDEN)
    if k_forms:
        place = jnp.zeros((2, HEAD_DIM, LANES), F32)
        place = place.at[0, :, :HEAD_DIM].set(jnp.eye(HEAD_DIM)).at[1, :, HEAD_DIM:].set(jnp.eye(HEAD_DIM))
        w2s = jnp.einsum("cd,ge,fdm->gcefm", w2, eye, place)
        w2s = w2s.reshape(N_KV_GROUPS * CMP_HIDDEN, N_KV_GROUPS * 2 * LANES)
    else:
        w2s = jnp.einsum("cd,ge->gced", w2, eye).reshape(N_KV_GROUPS * CMP_HIDDEN, KV_WIDTH)
    return pe_flat, w1s.astype(BF16), w2s.astype(BF16)


def kernel(x, norm1, w_in, w_pool, s_pool, cmp_pe_k, cmp_w1_k, cmp_w2_k, cmp_pe_v, cmp_w1_v, cmp_w2_v,
           norm_pool_out, norm_attn_out, w_out, norm2, w_gate, w_up, conv_w, conv_b, w_down, norm_f):
    B, T, D = x.shape
    depth = w_in.shape[0]
    nc = T // CMP_STRIDE
    row = lambda v: v.reshape(1, -1)
    n_gate = HEADS_PER_GROUP * N_BRANCHES
    for l in range(depth):
        w = w_in[l].astype(BF16)
        o_kv = POOL_WIDTH + ATTN_WIDTH
        o_g = o_kv + 6 * KV_WIDTH
        wg = jnp.zeros((D, N_KV_GROUPS * LANES), BF16)
        for g in range(N_KV_GROUPS):
            wg = wg.at[:, g * LANES:g * LANES + n_gate].set(w[:, o_g + g * n_gate:o_g + (g + 1) * n_gate])
        pool, q, kc, vc, gates, ksa, ksb, vst, kwa, kwb, vwt = _inproj(
            x, row(norm1[l]), w[:, :POOL_WIDTH], w[:, POOL_WIDTH:o_kv], w[:, o_kv:o_g], wg,
            w_pool[l].astype(BF16), row(s_pool[l]), row(norm_pool_out[l]))
        pek, w1k, w2k = _compress_weights(cmp_pe_k[l], cmp_w1_k[l], cmp_w2_k[l], True)
        pev, w1v, w2v = _compress_weights(cmp_pe_v[l], cmp_w1_v[l], cmp_w2_v[l], False)
        kcmp, vcmpt = _compress(kc.reshape(B, nc, CMP_STRIDE * KV_WIDTH), vc.reshape(B, nc, CMP_STRIDE * KV_WIDTH),
                                pek, pev, w1k, w1v, w2k, w2v)
        attn = _attention(q, gates, kcmp, vcmpt, ksa, ksb, vst, kwa, kwb, vwt)
        x = _ffn(x, pool, attn, row(norm_attn_out[l]), w_out[l].astype(BF16), row(norm2[l]),
                 w_gate[l].astype(BF16), w_up[l].astype(BF16), conv_w[l], row(conv_b[l]),
                 w_down[l].astype(BF16), row(norm_f), final=(l == depth - 1))
    return x
```

```python
import functools

import jax
import jax.numpy as jnp
from jax import lax
from jax.experimental import pallas as pl
from jax.experimental.pallas import tpu as pltpu

D_MODEL = 1024
POOL_WIDTH = 512
POOL_WINDOWS = (2, 4, 8, 16)
POOL_GROUP = 128
ATTN_WIDTH = 512
HEAD_DIM = 64
N_KV_GROUPS = 2
HEADS_PER_GROUP = 4
GROUP_WIDTH = HEADS_PER_GROUP * HEAD_DIM
KV_WIDTH = N_KV_GROUPS * HEAD_DIM
N_BRANCHES = 3
CMP_LEN = 32
CMP_STRIDE = 16
CMP_HIDDEN = 128
CMP_PER_SEL = 4
SEL_BLOCK = 64
SEL_TOPK = 16
WINDOW = 512
D_FF = 2816
EPS = 1e-6
NEG_INF = -1e30
FORCE_SCORE = 1e6
SCALE = HEAD_DIM ** -0.5
LOG2E = 1.4426950408889634

LANES = 128
SUBLANES = 8
BF16_ROWS = 16
MAX_SEL_BLOCKS = 64
SEL_MASK_BIAS = -(2.0 ** 100)
VT_ROWS = HEAD_DIM + BF16_ROWS
VMEM_LIMIT = 56 * 1024 * 1024

ROW_TILE = 512
Q_TILE = 256
SEL_KEY_TILE = 512
HALO = 16
FF_CHUNK = 1024

F32 = jnp.float32
BF16 = jnp.bfloat16
NT_DIMS = (((1,), (1,)), ((), ()))


def _rms(x, g):
    return x * lax.rsqrt(jnp.mean(x * x, axis=-1, keepdims=True) + EPS) * g


def _dot(a, b):
    return jnp.dot(a, b, preferred_element_type=F32)


def _dot_nt(a, b):
    return lax.dot_general(a, b, NT_DIMS, preferred_element_type=F32)


def _inproj_kernel(x_ref, n1_ref, wu_ref, wq_ref, wkv_ref, wg_ref, wpool_ref, spool_ref, npool_ref,
                   pool_ref, q_ref, kc_ref, vc_ref, gate_ref,
                   ksa_ref, ksb_ref, vst_ref, kwa_ref, kwb_ref, vwt_ref,
                   ext_ref):
    t = pl.program_id(1)
    tt = x_ref.shape[1]
    h = _rms(x_ref[0], n1_ref[...]).astype(BF16)

    q_ref[0] = (_dot(h, wq_ref[...]) * (SCALE * LOG2E)).astype(BF16)
    gate_ref[0] = jax.nn.sigmoid(_dot(h, wg_ref[...]))

    kv = _dot(h, wkv_ref[...])
    kc_ref[0] = kv[:, 0:KV_WIDTH]
    vc_ref[0] = kv[:, KV_WIDTH:2 * KV_WIDTH]

    lane = lax.broadcasted_iota(jnp.int32, (tt, LANES), 1)
    row = lax.broadcasted_iota(jnp.int32, (tt, LANES), 0)
    lo = lane < HEAD_DIM
    blk = (t * tt + row) // SEL_BLOCK
    onehot_hi = jnp.where(lane - HEAD_DIM == blk, 1.0, 0.0)
    onehot_lo = jnp.where(lane == blk, 1.0, 0.0)

    def emit_k(pair, a_ref, b_ref, fill_hi, fill_lo):
        rolled = pltpu.roll(pair, HEAD_DIM, axis=1)
        a_ref[0, 0] = jnp.where(lo, pair, fill_hi).astype(BF16)
        b_ref[0, 0] = jnp.where(lo, fill_lo, rolled).astype(BF16)
        a_ref[0, 1] = jnp.where(lo, rolled, fill_hi).astype(BF16)
        b_ref[0, 1] = jnp.where(lo, fill_lo, pair).astype(BF16)

    def emit_vt(pair, vt_ref):
        pt = pair.T.astype(BF16)
        for g in range(N_KV_GROUPS):
            vt_ref[0, g, 0:HEAD_DIM, :] = pt[g * HEAD_DIM:(g + 1) * HEAD_DIM]
            vt_ref[0, g, HEAD_DIM:VT_ROWS, :] = jnp.ones((BF16_ROWS, tt), BF16)

    emit_k(kv[:, 2 * KV_WIDTH:3 * KV_WIDTH], ksa_ref, ksb_ref, onehot_hi, onehot_lo)
    emit_vt(kv[:, 3 * KV_WIDTH:4 * KV_WIDTH], vst_ref)
    emit_k(kv[:, 4 * KV_WIDTH:5 * KV_WIDTH], kwa_ref, kwb_ref, 0.0, 0.0)
    emit_vt(kv[:, 5 * KV_WIDTH:6 * KV_WIDTH], vwt_ref)

    u = _dot(h, wu_ref[...])

    @pl.when(t == 0)
    def _():
        ext_ref[0:HALO, :] = jnp.zeros((HALO, POOL_WIDTH), F32)

    @pl.when(t > 0)
    def _():
        ext_ref[0:HALO, :] = ext_ref[tt:tt + HALO, :]

    ext_ref[HALO:HALO + tt, :] = u
    pos = t * tt + lax.broadcasted_iota(jnp.int32, (tt, 1), 0)
    mixed = []
    for p, w in enumerate(POOL_WINDOWS):
        cols = slice(p * POOL_GROUP, (p + 1) * POOL_GROUP)
        acc = u[:, cols]
        for k in range(1, w):
            acc = acc + ext_ref[HALO - k:HALO - k + tt, cols]
        cnt = jnp.minimum(pos + 1, w).astype(F32)
        pooled = acc / cnt - u[:, cols]
        mixed.append(_dot(pooled.astype(BF16), wpool_ref[p]))
    mixed = jnp.concatenate(mixed, axis=1) * spool_ref[...]
    pool_ref[0] = _rms(mixed, npool_ref[...])


def _inproj(x, n1, wu, wq, wkv, wg, wpool, spool, npool):
    B, T, D = x.shape
    tt = min(ROW_TILE, T)
    row3 = lambda w: pl.BlockSpec((1, tt, w), lambda b, t: (b, t, 0))
    k_spec = pl.BlockSpec((1, N_KV_GROUPS, tt, LANES), lambda b, t: (b, 0, t, 0))
    vt_spec = pl.BlockSpec((1, N_KV_GROUPS, VT_ROWS, tt), lambda b, t: (b, 0, 0, t))
    full = lambda a: pl.BlockSpec(a.shape, lambda b, t: (0,) * a.ndim)
    k_shape = jax.ShapeDtypeStruct((B, N_KV_GROUPS, T, LANES), BF16)
    vt_shape = jax.ShapeDtypeStruct((B, N_KV_GROUPS, VT_ROWS, T), BF16)
    return pl.pallas_call(
        _inproj_kernel,
        grid=(B, T // tt),
        in_specs=[row3(D), full(n1), full(wu), full(wq), full(wkv), full(wg), full(wpool), full(spool),
                  full(npool)],
        out_specs=[row3(POOL_WIDTH), row3(ATTN_WIDTH), row3(KV_WIDTH), row3(KV_WIDTH),
                   row3(N_KV_GROUPS * LANES), k_spec, k_spec, vt_spec, k_spec, k_spec, vt_spec],
        out_shape=[jax.ShapeDtypeStruct((B, T, POOL_WIDTH), F32),
                   jax.ShapeDtypeStruct((B, T, ATTN_WIDTH), BF16),
                   jax.ShapeDtypeStruct((B, T, KV_WIDTH), F32),
                   jax.ShapeDtypeStruct((B, T, KV_WIDTH), F32),
                   jax.ShapeDtypeStruct((B, T, N_KV_GROUPS * LANES), F32),
                   k_shape, k_shape, vt_shape, k_shape, k_shape, vt_shape],
        scratch_shapes=[pltpu.VMEM((HALO + tt, POOL_WIDTH), F32)],
        compiler_params=pltpu.CompilerParams(dimension_semantics=("arbitrary", "arbitrary"),
                                             vmem_limit_bytes=VMEM_LIMIT),
        name="inproj_pool",
    )(x, n1, wu, wq, wkv, wg, wpool, spool, npool)


def _compress_kernel(kc_ref, vc_ref, pek_ref, pev_ref, w1k_ref, w1v_ref, w2k_ref, w2v_ref,
                     kcmp_ref, vcmpt_ref):
    nc = kc_ref.shape[1]

    def hidden(c_ref, pe_ref, w1_ref):
        c = c_ref[0]
        top = _dot((c + pe_ref[0:1, :]).astype(BF16), w1_ref[0])
        bot = _dot((c + pe_ref[1:2, :]).astype(BF16), w1_ref[1])
        return jax.nn.gelu(top + pltpu.roll(bot, nc - 1, axis=0)).astype(BF16)

    kk = _dot(hidden(kc_ref, pek_ref, w1k_ref), w2k_ref[...])
    for i in range(2 * N_KV_GROUPS):
        kcmp_ref[0, i] = kk[:, i * LANES:(i + 1) * LANES].astype(BF16)
    vv = _dot(hidden(vc_ref, pev_ref, w1v_ref), w2v_ref[...])
    vcmpt_ref[0] = vv.T.astype(BF16)


def _compress(kc2, vc2, pek, pev, w1k, w1v, w2k, w2v):
    B, nc, width = kc2.shape
    full = lambda a: pl.BlockSpec(a.shape, lambda b: (0,) * a.ndim)
    chunk = pl.BlockSpec((1, nc, width), lambda b: (b, 0, 0))
    return pl.pallas_call(
        _compress_kernel,
        grid=(B,),
        in_specs=[chunk, chunk, full(pek), full(pev), full(w1k), full(w1v), full(w2k), full(w2v)],
        out_specs=[pl.BlockSpec((1, 2 * N_KV_GROUPS, nc, LANES), lambda b: (b, 0, 0, 0)),
                   pl.BlockSpec((1, KV_WIDTH, nc), lambda b: (b, 0, 0))],
        out_shape=[jax.ShapeDtypeStruct((B, 2 * N_KV_GROUPS, nc, LANES), BF16),
                   jax.ShapeDtypeStruct((B, KV_WIDTH, nc), BF16)],
        compiler_params=pltpu.CompilerParams(dimension_semantics=("arbitrary",),
                                             vmem_limit_bytes=VMEM_LIMIT),
        name="compress",
    )(kc2, vc2, pek, pev, w1k, w1v, w2k, w2v)


def _attn_kernel(q_ref, gate_ref, kc_ref, vct_ref, ksa_ref, ksb_ref, vst_ref, kwa_ref, kwb_ref, vwt_ref,
                 o_ref, psum_ref, bias_ref, owin_ref, osel_ref, *, topk):
    qt = pl.program_id(2)
    tq = q_ref.shape[1]
    nc = kc_ref.shape[2]
    ns = nc // CMP_PER_SEL
    t0 = qt * tq
    t_lane = t0 + lax.broadcasted_iota(jnp.int32, (1, tq), 1)

    def normalized(acc):
        return acc[0:HEAD_DIM] / acc[HEAD_DIM:HEAD_DIM + 1]

    q_pairs = jnp.concatenate([q_ref[0, :, k * LANES:(k + 1) * LANES] for k in range(HEADS_PER_GROUP // 2)],
                              axis=0)
    span = WINDOW + tq
    w_keys = pl.ds(pl.multiple_of(jnp.maximum(t0 - WINDOW, 0), tq), span)

    n_sub = lax.broadcasted_iota(jnp.int32, (nc, 1), 0)
    cbias = jnp.where(n_sub * CMP_STRIDE + (CMP_LEN - 1) <= t_lane, 0.0, NEG_INF)
    any_visible = jnp.where(t_lane >= CMP_LEN - 1, 1.0, 0.0)
    cbias2 = jnp.concatenate([cbias, cbias], axis=1)
    visible2 = jnp.concatenate([any_visible, any_visible], axis=1)
    psum = jnp.zeros((nc, tq), F32)
    o_cmp = [None] * HEADS_PER_GROUP
    s_cmp = [_dot_nt(kc_ref[0, f], q_pairs) for f in range(2)]
    for f in range(2):
        s = s_cmp[f] + cbias2
        e = jnp.exp2(s - jnp.max(s, axis=0, keepdims=True))
        p = e * (visible2 / jnp.sum(e, axis=0, keepdims=True))
        psum = psum + p[:, 0:tq] + p[:, tq:2 * tq]
        o = _dot(vct_ref[0], p.astype(BF16))
        o_cmp[f], o_cmp[f + 2] = o[:, 0:tq], o[:, tq:2 * tq]
    for i in range(tq // LANES):
        psum_ref[i] = psum[:, i * LANES:(i + 1) * LANES]

    j_sub = lax.broadcasted_iota(jnp.int32, (ns, 1), 0)
    parts = [jnp.concatenate([psum_ref[i, pl.ds(k, ns, stride=CMP_PER_SEL), :] for i in range(tq // LANES)],
                             axis=1) for k in range(CMP_PER_SEL)]
    before = jnp.where(j_sub == 0, 0.0, pltpu.roll(parts[CMP_PER_SEL - 1], 1, axis=0))
    imp = parts[0] + parts[1] + parts[2] + parts[3] + before
    cur = t_lane // SEL_BLOCK
    forced = (j_sub == 0) | (j_sub == cur) | (j_sub == cur - 1)
    valid = j_sub * SEL_BLOCK <= t_lane
    imp = jnp.where(forced, FORCE_SCORE, imp)
    imp = jnp.where(valid, imp, NEG_INF)

    n_chunks = ns // SUBLANES
    chunks = [imp[c * SUBLANES:(c + 1) * SUBLANES] for c in range(n_chunks)]
    sub8 = lax.broadcasted_iota(jnp.int32, (SUBLANES, 1), 0)
    bias_ref[...] = jnp.full(bias_ref.shape, SEL_MASK_BIAS, F32)
    causal_chunks = (t0 + tq - 1) // (SEL_BLOCK * SUBLANES) + 1
    for nv in range(1, n_chunks + 1):
        @pl.when(causal_chunks == nv)
        def _():
            ranks = [jnp.zeros((SUBLANES, tq), F32) for _ in range(nv)]
            for jp in range(nv * SUBLANES):
                cj, rj = divmod(jp, SUBLANES)
                other = chunks[cj][rj:rj + 1, :]
                for c in range(nv):
                    if c > cj:
                        before_me = jnp.where(other >= chunks[c], 1.0, 0.0)
                    elif c < cj:
                        before_me = jnp.where(other > chunks[c], 1.0, 0.0)
                    else:
                        before_me = jnp.where(sub8 > rj, jnp.where(other >= chunks[c], 1.0, 0.0),
                                              jnp.where(other > chunks[c], 1.0, 0.0))
                    ranks[c] = ranks[c] + before_me
            for c in range(nv):
                rows = slice(c * SUBLANES, (c + 1) * SUBLANES)
                bias_ref[rows, :] = jnp.where(valid[rows], jnp.where(ranks[c] < topk, 0.0, SEL_MASK_BIAS),
                                              SEL_MASK_BIAS)
    bias_t = bias_ref[...]
    bias = jnp.concatenate([bias_t, bias_t], axis=0).T

    lane = lax.broadcasted_iota(jnp.int32, (tq, LANES), 1)
    lo = lane < HEAD_DIM
    qa, qb = [], []
    for k in range(HEADS_PER_GROUP // 2):
        qs = q_ref[0, :, k * LANES:(k + 1) * LANES].astype(F32)
        qa.append(jnp.where(lo, qs, bias).astype(BF16))
        qb.append(jnp.where(lo, bias, qs).astype(BF16))
    q_forms = (jnp.concatenate(qa, axis=0), jnp.concatenate(qb, axis=0))
    k_sel = (ksa_ref, ksb_ref)
    kt_sel = min(SEL_KEY_TILE, ksa_ref.shape[2])

    def window_and_selected(n_past):
        kpos_w = w_keys.start + lax.broadcasted_iota(jnp.int32, (span, 1), 0)
        wbias = jnp.where(kpos_w <= t_lane, jnp.where(kpos_w > t_lane - WINDOW, 0.0, NEG_INF), NEG_INF)
        kpos_d = n_past * kt_sel + lax.broadcasted_iota(jnp.int32, (kt_sel, 1), 0)
        causal = jnp.where(kpos_d <= t_lane, 0.0, NEG_INF)
        twice = lambda a: jnp.concatenate([a, a], axis=1)
        sel_keys = lambda tile: slice(tile * kt_sel, (tile + 1) * kt_sel)
        jobs = [("win", (kwa_ref, kwb_ref), vwt_ref, w_keys, (q_pairs, q_pairs), twice(wbias)),
                ("sel", k_sel, vst_ref, sel_keys(n_past), q_forms, twice(causal))]
        jobs += [("sel", k_sel, vst_ref, sel_keys(tile), q_forms, None) for tile in range(n_past)]

        def scores(job, zeros):
            _, k_forms, _, keys, q, _ = job
            return [_dot_nt(k_forms[f][0, 0, keys, :], q[f] if zeros is None else q[f] + zeros[f])
                    for f in range(2)]

        m, acc = {}, {}
        s_next = scores(jobs[0], None)
        for i, job in enumerate(jobs):
            branch, _, vt_ref, keys, _, bias = job
            s_cur = s_next
            s, m_old, m_new = [None, None], [None, None], [None, None]
            for f in range(2):
                s[f] = s_cur[f] if bias is None else s_cur[f] + bias
                tile_max = jnp.max(s[f], axis=0, keepdims=True)
                m_old[f] = m.get((branch, f))
                m_new[f] = tile_max if m_old[f] is None else jnp.maximum(m_old[f], tile_max)
            if i + 1 < len(jobs):
                zeros = [(m_new[f][:, 0:LANES] * 0.0).astype(BF16) for f in range(2)]
                s_next = scores(jobs[i + 1], zeros)
            for f in range(2):
                pv = _dot(vt_ref[0, 0, :, keys], jnp.exp2(s[f] - m_new[f]).astype(BF16))
                if m_old[f] is None:
                    acc[branch, f] = pv
                else:
                    acc[branch, f] = jnp.exp2(m_old[f] - m_new[f]) * acc[branch, f] + pv
                m[branch, f] = m_new[f]
        return acc

    for n_past in range(ksa_ref.shape[2] // kt_sel):
        @pl.when(t0 // kt_sel == n_past)
        def _():
            acc = window_and_selected(n_past)
            for f in range(2):
                owin_ref[f] = normalized(acc["win", f])
                osel_ref[f] = normalized(acc["sel", f])
    o_win = [owin_ref[f] for f in range(2)]
    o_sel = [osel_ref[f] for f in range(2)]

    gates_t = gate_ref[0].T
    for k in range(HEADS_PER_GROUP // 2):
        heads = []
        for r in (2 * k, 2 * k + 1):
            cols = slice((r // 2) * tq, (r // 2 + 1) * tq)
            branches = (o_cmp[r], o_sel[r % 2][:, cols], o_win[r % 2][:, cols])
            out = jnp.zeros((HEAD_DIM, tq), F32)
            for br in range(N_BRANCHES):
                c = r * N_BRANCHES + br
                out = out + gates_t[c:c + 1, :] * branches[br]
            heads.append(out)
        o_ref[0, :, k * LANES:(k + 1) * LANES] = jnp.concatenate(heads, axis=0).T


def _attention(q, gates, kcmp, vcmpt, ksa, ksb, vst, kwa, kwb, vwt):
    B, T, _ = q.shape
    nc = kcmp.shape[2]
    tq = min(Q_TILE, T)
    assert T % tq == 0 and T >= WINDOW + tq and T // SEL_BLOCK <= MAX_SEL_BLOCKS
    topk = min(SEL_TOPK, T // SEL_BLOCK)
    k_spec = pl.BlockSpec((1, 1, T, LANES), lambda b, g, t: (b, g, 0, 0))
    vt_spec = pl.BlockSpec((1, 1, VT_ROWS, T), lambda b, g, t: (b, g, 0, 0))
    return pl.pallas_call(
        functools.partial(_attn_kernel, topk=topk),
        grid=(B, N_KV_GROUPS, T // tq),
        in_specs=[pl.BlockSpec((1, tq, GROUP_WIDTH), lambda b, g, t: (b, t, g)),
                  pl.BlockSpec((1, tq, LANES), lambda b, g, t: (b, t, g)),
                  pl.BlockSpec((1, 2, nc, LANES), lambda b, g, t: (b, g, 0, 0)),
                  pl.BlockSpec((1, HEAD_DIM, nc), lambda b, g, t: (b, g, 0)),
                  k_spec, k_spec, vt_spec, k_spec, k_spec, vt_spec],
        out_specs=pl.BlockSpec((1, tq, GROUP_WIDTH), lambda b, g, t: (b, t, g)),
        out_shape=jax.ShapeDtypeStruct((B, T, ATTN_WIDTH), F32),
        scratch_shapes=[pltpu.VMEM((tq // LANES, nc, LANES), F32),
                        pltpu.VMEM((MAX_SEL_BLOCKS, tq), F32),
                        pltpu.VMEM((2, HEAD_DIM, 2 * tq), F32),
                        pltpu.VMEM((2, HEAD_DIM, 2 * tq), F32)],
        compiler_params=pltpu.CompilerParams(dimension_semantics=("arbitrary",) * 3,
                                             vmem_limit_bytes=VMEM_LIMIT),
        name="nsa_attention",
    )(q, gates, kcmp, vcmpt, ksa, ksb, vst, kwa, kwb, vwt)


def _ffn_kernel(x_ref, pool_ref, attn_ref, nattn_ref, wo_ref, n2_ref, wg_ref, wu_ref, cw_ref, cb_ref,
                wd_ref, nf_ref, o_ref, prev_ref, *, final):
    t = pl.program_id(1)
    tt = x_ref.shape[1]
    attn = _rms(attn_ref[0], nattn_ref[...]).astype(BF16)
    x1 = (x_ref[0] + _dot(pool_ref[0].astype(BF16), wo_ref[0:POOL_WIDTH, :])
          + _dot(attn, wo_ref[POOL_WIDTH:, :]))
    h = _rms(x1, n2_ref[...]).astype(BF16)

    @pl.when(t == 0)
    def _():
        prev_ref[...] = jnp.zeros(prev_ref.shape, F32)

    row = lax.broadcasted_iota(jnp.int32, (tt, 1), 0)
    o_ref[0] = x1
    for c0 in range(0, D_FF, FF_CHUNK):
        cols = slice(c0, min(c0 + FF_CHUNK, D_FF))
        g = _dot(h, wg_ref[:, cols])
        p1 = prev_ref[7:8, cols]
        p2 = prev_ref[6:7, cols]
        g1 = jnp.where(row == 0, p1, pltpu.roll(g, 1, axis=0))
        g2 = jnp.where(row == 0, p2, jnp.where(row == 1, p1, pltpu.roll(g, 2, axis=0)))
        prev_ref[:, cols] = g[tt - 8:tt, :]
        gc = (g * cw_ref[2:3, cols] + g1 * cw_ref[1:2, cols] + g2 * cw_ref[0:1, cols]
              + cb_ref[:, cols])
        act = (jax.nn.silu(gc) * _dot(h, wu_ref[:, cols])).astype(BF16)
        o_ref[0] += _dot(act, wd_ref[cols, :])
    if final:
        o_ref[0] = _rms(o_ref[0], nf_ref[...])


def _ffn(x, pool, attn, nattn, wo, n2, wg, wu, cw, cb, wd, nf, final):
    B, T, D = x.shape
    tt = min(ROW_TILE, T)
    row3 = lambda w: pl.BlockSpec((1, tt, w), lambda b, t: (b, t, 0))
    full = lambda a: pl.BlockSpec(a.shape, lambda b, t: (0,) * a.ndim, pipeline_mode=pl.Buffered(1))
    return pl.pallas_call(
        functools.partial(_ffn_kernel, final=final),
        grid=(B, T // tt),
        in_specs=[row3(D), row3(POOL_WIDTH), row3(ATTN_WIDTH), full(nattn), full(wo), full(n2), full(wg),
                  full(wu), full(cw), full(cb), full(wd), full(nf)],
        out_specs=row3(D),
        out_shape=jax.ShapeDtypeStruct((B, T, D), F32),
        scratch_shapes=[pltpu.VMEM((8, D_FF), F32)],
        compiler_params=pltpu.CompilerParams(dimension_semantics=("arbitrary", "arbitrary"),
                                             vmem_limit_bytes=VMEM_LIMIT),
        name="outproj_ffn",
    )(x, pool, attn, nattn, wo, n2, wg, wu, cw, cb, wd, nf)


def _compress_weights(pe, w1, w2, k_forms):
    half = CMP_LEN // 2
    eye = jnp.eye(N_KV_GROUPS, dtype=F32)
    pe_flat = jnp.broadcast_to(pe.reshape(2, half, 1, HEAD_DIM), (2, half, N_KV_GROUPS, HEAD_DIM))
    pe_flat = pe_flat.reshape(2, half * KV_WIDTH)
    w1r = w1.reshape(2, half, HEAD_DIM, CMP_HIDDEN)
    w1s = jnp.einsum("hldc,ge->hlgdec", w1r, eye).reshape(2, half * KV_WIDTH, N_KV_GROUPS * CMP_HIDDEN)
    if k_forms:
        place = jnp.zeros((2, HEAD_DIM, LANES), F32)
        place = place.at[0, :, :HEAD_DIM].set(jnp.eye(HEAD_DIM)).at[1, :, HEAD_DIM:].set(jnp.eye(HEAD_DIM))
        w2s = jnp.einsum("cd,ge,fdm->gcefm", w2, eye, place)
        w2s = w2s.reshape(N_KV_GROUPS * CMP_HIDDEN, N_KV_GROUPS * 2 * LANES)
    else:
        w2s = jnp.einsum("cd,ge->gced", w2, eye).reshape(N_KV_GROUPS * CMP_HIDDEN, KV_WIDTH)
    return pe_flat, w1s.astype(BF16), w2s.astype(BF16)


def kernel(x, norm1, w_in, w_pool, s_pool, cmp_pe_k, cmp_w1_k, cmp_w2_k, cmp_pe_v, cmp_w1_v, cmp_w2_v,
           norm_pool_out, norm_attn_out, w_out, norm2, w_gate, w_up, conv_w, conv_b, w_down, norm_f):
    B, T, D = x.shape
    depth = w_in.shape[0]
    nc = T // CMP_STRIDE
    row = lambda v: v.reshape(1, -1)
    n_gate = HEADS_PER_GROUP * N_BRANCHES
    for l in range(depth):
        w = w_in[l].astype(BF16)
        o_kv = POOL_WIDTH + ATTN_WIDTH
        o_g = o_kv + 6 * KV_WIDTH
        wg = jnp.zeros((D, N_KV_GROUPS * LANES), BF16)
        for g in range(N_KV_GROUPS):
            wg = wg.at[:, g * LANES:g * LANES + n_gate].set(w[:, o_g + g * n_gate:o_g + (g + 1) * n_gate])
        pool, q, kc, vc, gates, ksa, ksb, vst, kwa, kwb, vwt = _inproj(
            x, row(norm1[l]), w[:, :POOL_WIDTH], w[:, POOL_WIDTH:o_kv], w[:, o_kv:o_g], wg,
            w_pool[l].astype(BF16), row(s_pool[l]), row(norm_pool_out[l]))
        pek, w1k, w2k = _compress_weights(cmp_pe_k[l], cmp_w1_k[l], cmp_w2_k[l], True)
        pev, w1v, w2v = _compress_weights(cmp_pe_v[l], cmp_w1_v[l], cmp_w2_v[l], False)
        kcmp, vcmpt = _compress(kc.reshape(B, nc, CMP_STRIDE * KV_WIDTH), vc.reshape(B, nc, CMP_STRIDE * KV_WIDTH),
                                pek, pev, w1k, w1v, w2k, w2v)
        attn = _attention(q, gates, kcmp, vcmpt, ksa, ksb, vst, kwa, kwb, vwt)
        x = _ffn(x, pool, attn, row(norm_attn_out[l]), w_out[l].astype(BF16), row(norm2[l]),
                 w_gate[l].astype(BF16), w_up[l].astype(BF16), conv_w[l], row(conv_b[l]),
                 w_down[l].astype(BF16), row(norm_f), final=(l == depth - 1))
    return x
```

```python
import functools

import jax
import jax.numpy as jnp
from jax import lax
from jax.experimental import pallas as pl
from jax.experimental.pallas import tpu as pltpu

D_MODEL = 1024
POOL_WIDTH = 512
POOL_WINDOWS = (2, 4, 8, 16)
POOL_GROUP = 128
ATTN_WIDTH = 512
HEAD_DIM = 64
N_KV_GROUPS = 2
HEADS_PER_GROUP = 4
GROUP_WIDTH = HEADS_PER_GROUP * HEAD_DIM
KV_WIDTH = N_KV_GROUPS * HEAD_DIM
N_BRANCHES = 3
CMP_LEN = 32
CMP_STRIDE = 16
CMP_HIDDEN = 128
CMP_PER_SEL = 4
SEL_BLOCK = 64
SEL_TOPK = 16
WINDOW = 512
D_FF = 2816
EPS = 1e-6
NEG_INF = -1e30
FORCE_SCORE = 1e6
SCALE = HEAD_DIM ** -0.5
LOG2E = 1.4426950408889634

LANES = 128
SUBLANES = 8
BF16_ROWS = 16
MAX_SEL_BLOCKS = 64
SEL_MASK_BIAS = -(2.0 ** 100)
VT_ROWS = HEAD_DIM + BF16_ROWS
VMEM_LIMIT = 56 * 1024 * 1024

ROW_TILE = 512
PROJ_PARTS = 2
Q_TILE = 256
SEL_KEY_TILE = 512
HALO = 16
FF_CHUNK = 1024

F32 = jnp.float32
BF16 = jnp.bfloat16
NT_DIMS = (((1,), (1,)), ((), ()))


def _rms(x, g):
    return x * lax.rsqrt(jnp.mean(x * x, axis=-1, keepdims=True) + EPS) * g


def _dot(a, b):
    return jnp.dot(a, b, preferred_element_type=F32)


def _dot_nt(a, b):
    return lax.dot_general(a, b, NT_DIMS, preferred_element_type=F32)


def _inproj_kernel(x_ref, n1_ref, wu_ref, wq_ref, wkv_ref, wg_ref, wpool_ref, spool_ref, npool_ref,
                   pool_ref, q_ref, kc_ref, vc_ref, gate_ref,
                   ksa_ref, ksb_ref, vst_ref, kwa_ref, kwb_ref, vwt_ref,
                   ext_ref):
    t = pl.program_id(1)
    tt = x_ref.shape[1]

    @pl.when(t == 0)
    def _():
        ext_ref[0:HALO, :] = jnp.zeros((HALO, POOL_WIDTH), F32)

    @pl.when(t > 0)
    def _():
        ext_ref[0:HALO, :] = ext_ref[tt:tt + HALO, :]

    part = tt // PROJ_PARTS
    lane = lax.broadcasted_iota(jnp.int32, (part, LANES), 1)
    row = lax.broadcasted_iota(jnp.int32, (part, LANES), 0)
    lo = lane < HEAD_DIM
    for i in range(PROJ_PARTS):
        rows = slice(i * part, (i + 1) * part)
        h = _rms(x_ref[0, rows, :], n1_ref[...]).astype(BF16)
        q_ref[0, rows, :] = (_dot(h, wq_ref[...]) * (SCALE * LOG2E)).astype(BF16)
        gate_ref[0, rows, :] = jax.nn.sigmoid(_dot(h, wg_ref[...]))
        kv = _dot(h, wkv_ref[...])
        kc_ref[0, rows, :] = kv[:, 0:KV_WIDTH]
        vc_ref[0, rows, :] = kv[:, KV_WIDTH:2 * KV_WIDTH]
        ext_ref[HALO + i * part:HALO + (i + 1) * part, :] = _dot(h, wu_ref[...])

        blk = (t * tt + i * part + row) // SEL_BLOCK
        onehot_hi = jnp.where(lane - HEAD_DIM == blk, 1.0, 0.0)
        onehot_lo = jnp.where(lane == blk, 1.0, 0.0)

        def emit_k(pair, a_ref, b_ref, fill_hi, fill_lo):
            rolled = pltpu.roll(pair, HEAD_DIM, axis=1)
            a_ref[0, 0, rows, :] = jnp.where(lo, pair, fill_hi).astype(BF16)
            b_ref[0, 0, rows, :] = jnp.where(lo, fill_lo, rolled).astype(BF16)
            a_ref[0, 1, rows, :] = jnp.where(lo, rolled, fill_hi).astype(BF16)
            b_ref[0, 1, rows, :] = jnp.where(lo, fill_lo, pair).astype(BF16)

        def emit_vt(pair, vt_ref):
            pt = pair.T.astype(BF16)
            for g in range(N_KV_GROUPS):
                vt_ref[0, g, 0:HEAD_DIM, rows] = pt[g * HEAD_DIM:(g + 1) * HEAD_DIM]
                vt_ref[0, g, HEAD_DIM:VT_ROWS, rows] = jnp.ones((BF16_ROWS, part), BF16)

        emit_k(kv[:, 2 * KV_WIDTH:3 * KV_WIDTH], ksa_ref, ksb_ref, onehot_hi, onehot_lo)
        emit_vt(kv[:, 3 * KV_WIDTH:4 * KV_WIDTH], vst_ref)
        emit_k(kv[:, 4 * KV_WIDTH:5 * KV_WIDTH], kwa_ref, kwb_ref, 0.0, 0.0)
        emit_vt(kv[:, 5 * KV_WIDTH:6 * KV_WIDTH], vwt_ref)

    pos = t * tt + lax.broadcasted_iota(jnp.int32, (tt, 1), 0)
    mixed = []
    for p, w in enumerate(POOL_WINDOWS):
        cols = slice(p * POOL_GROUP, (p + 1) * POOL_GROUP)
        u = ext_ref[HALO:HALO + tt, cols]
        acc = u
        for k in range(1, w):
            acc = acc + ext_ref[HALO - k:HALO - k + tt, cols]
        cnt = jnp.minimum(pos + 1, w).astype(F32)
        pooled = acc / cnt - u
        mixed.append(_dot(pooled.astype(BF16), wpool_ref[p]))
    mixed = jnp.concatenate(mixed, axis=1) * spool_ref[...]
    pool_ref[0] = _rms(mixed, npool_ref[...])


def _inproj(x, n1, wu, wq, wkv, wg, wpool, spool, npool):
    B, T, D = x.shape
    tt = min(ROW_TILE, T)
    row3 = lambda w: pl.BlockSpec((1, tt, w), lambda b, t: (b, t, 0))
    k_spec = pl.BlockSpec((1, N_KV_GROUPS, tt, LANES), lambda b, t: (b, 0, t, 0))
    vt_spec = pl.BlockSpec((1, N_KV_GROUPS, VT_ROWS, tt), lambda b, t: (b, 0, 0, t))
    full = lambda a: pl.BlockSpec(a.shape, lambda b, t: (0,) * a.ndim)
    k_shape = jax.ShapeDtypeStruct((B, N_KV_GROUPS, T, LANES), BF16)
    vt_shape = jax.ShapeDtypeStruct((B, N_KV_GROUPS, VT_ROWS, T), BF16)
    return pl.pallas_call(
        _inproj_kernel,
        grid=(B, T // tt),
        in_specs=[row3(D), full(n1), full(wu), full(wq), full(wkv), full(wg), full(wpool), full(spool),
                  full(npool)],
        out_specs=[row3(POOL_WIDTH), row3(ATTN_WIDTH), row3(KV_WIDTH), row3(KV_WIDTH),
                   row3(N_KV_GROUPS * LANES), k_spec, k_spec, vt_spec, k_spec, k_spec, vt_spec],
        out_shape=[jax.ShapeDtypeStruct((B, T, POOL_WIDTH), F32),
                   jax.ShapeDtypeStruct((B, T, ATTN_WIDTH), BF16),
                   jax.ShapeDtypeStruct((B, T, KV_WIDTH), F32),
                   jax.ShapeDtypeStruct((B, T, KV_WIDTH), F32),
                   jax.ShapeDtypeStruct((B, T, N_KV_GROUPS * LANES), F32),
                   k_shape, k_shape, vt_shape, k_shape, k_shape, vt_shape],
        scratch_shapes=[pltpu.VMEM((HALO + tt, POOL_WIDTH), F32)],
        compiler_params=pltpu.CompilerParams(dimension_semantics=("arbitrary", "arbitrary"),
                                             vmem_limit_bytes=VMEM_LIMIT),
        name="inproj_pool",
    )(x, n1, wu, wq, wkv, wg, wpool, spool, npool)


def _compress_kernel(kc_ref, vc_ref, pek_ref, pev_ref, w1k_ref, w1v_ref, w2k_ref, w2v_ref,
                     kcmp_ref, vcmpt_ref):
    nc = kc_ref.shape[1]

    def hidden(c_ref, pe_ref, w1_ref):
        c = c_ref[0]
        top = _dot((c + pe_ref[0:1, :]).astype(BF16), w1_ref[0])
        bot = _dot((c + pe_ref[1:2, :]).astype(BF16), w1_ref[1])
        return jax.nn.gelu(top + pltpu.roll(bot, nc - 1, axis=0)).astype(BF16)

    kk = _dot(hidden(kc_ref, pek_ref, w1k_ref), w2k_ref[...])
    for i in range(2 * N_KV_GROUPS):
        kcmp_ref[0, i] = kk[:, i * LANES:(i + 1) * LANES].astype(BF16)
    vv = _dot(hidden(vc_ref, pev_ref, w1v_ref), w2v_ref[...])
    vcmpt_ref[0] = vv.T.astype(BF16)


def _compress(kc2, vc2, pek, pev, w1k, w1v, w2k, w2v):
    B, nc, width = kc2.shape
    full = lambda a: pl.BlockSpec(a.shape, lambda b: (0,) * a.ndim)
    chunk = pl.BlockSpec((1, nc, width), lambda b: (b, 0, 0))
    return pl.pallas_call(
        _compress_kernel,
        grid=(B,),
        in_specs=[chunk, chunk, full(pek), full(pev), full(w1k), full(w1v), full(w2k), full(w2v)],
        out_specs=[pl.BlockSpec((1, 2 * N_KV_GROUPS, nc, LANES), lambda b: (b, 0, 0, 0)),
                   pl.BlockSpec((1, KV_WIDTH, nc), lambda b: (b, 0, 0))],
        out_shape=[jax.ShapeDtypeStruct((B, 2 * N_KV_GROUPS, nc, LANES), BF16),
                   jax.ShapeDtypeStruct((B, KV_WIDTH, nc), BF16)],
        compiler_params=pltpu.CompilerParams(dimension_semantics=("arbitrary",),
                                             vmem_limit_bytes=VMEM_LIMIT),
        name="compress",
    )(kc2, vc2, pek, pev, w1k, w1v, w2k, w2v)


def _attn_kernel(q_ref, gate_ref, kc_ref, vct_ref, ksa_ref, ksb_ref, vst_ref, kwa_ref, kwb_ref, vwt_ref,
                 o_ref, psum_ref, *, topk):
    qt = pl.program_id(2)
    tq = q_ref.shape[1]
    nc = kc_ref.shape[2]
    ns = nc // CMP_PER_SEL
    kt_sel = min(SEL_KEY_TILE, ksa_ref.shape[2])
    t0 = qt * tq
    t_lane = t0 + lax.broadcasted_iota(jnp.int32, (1, tq), 1)
    k_sel = (ksa_ref, ksb_ref)
    k_win = (kwa_ref, kwb_ref)
    span = WINDOW + tq
    w_keys = pl.ds(pl.multiple_of(jnp.maximum(t0 - WINDOW, 0), tq), span)

    def twice(a):
        return jnp.concatenate([a, a], axis=1)

    def normalized(acc):
        return acc[0:HEAD_DIM] / acc[HEAD_DIM:HEAD_DIM + 1]

    def step(n_past):
        q_pairs = jnp.concatenate([q_ref[0, :, k * LANES:(k + 1) * LANES]
                                   for k in range(HEADS_PER_GROUP // 2)], axis=0)
        s_win = [_dot_nt(k_win[f][0, 0, w_keys, :], q_pairs) for f in range(2)]
        s_cmp = [_dot_nt(kc_ref[0, f], q_pairs) for f in range(2)]

        n_sub = lax.broadcasted_iota(jnp.int32, (nc, 1), 0)
        cbias2 = twice(jnp.where(n_sub * CMP_STRIDE + (CMP_LEN - 1) <= t_lane, 0.0, NEG_INF))
        visible2 = twice(jnp.where(t_lane >= CMP_LEN - 1, 1.0, 0.0))
        psum = jnp.zeros((nc, tq), F32)
        o_cmp = [None] * HEADS_PER_GROUP
        for f in range(2):
            s = s_cmp[f] + cbias2
            e = jnp.exp2(s - jnp.max(s, axis=0, keepdims=True))
            p = e * (visible2 / jnp.sum(e, axis=0, keepdims=True))
            psum = psum + p[:, 0:tq] + p[:, tq:2 * tq]
            o = _dot(vct_ref[0], p.astype(BF16))
            o_cmp[f], o_cmp[f + 2] = o[:, 0:tq], o[:, tq:2 * tq]
        for i in range(tq // LANES):
            psum_ref[i] = psum[:, i * LANES:(i + 1) * LANES]

        j_sub = lax.broadcasted_iota(jnp.int32, (ns, 1), 0)
        parts = [jnp.concatenate([psum_ref[i, pl.ds(k, ns, stride=CMP_PER_SEL), :]
                                  for i in range(tq // LANES)], axis=1) for k in range(CMP_PER_SEL)]
        before = jnp.where(j_sub == 0, 0.0, pltpu.roll(parts[CMP_PER_SEL - 1], 1, axis=0))
        imp = parts[0] + parts[1] + parts[2] + parts[3] + before
        cur = t_lane // SEL_BLOCK
        forced = (j_sub == 0) | (j_sub == cur) | (j_sub == cur - 1)
        valid = j_sub * SEL_BLOCK <= t_lane
        imp = jnp.where(forced, FORCE_SCORE, imp)
        imp = jnp.where(valid, imp, NEG_INF)

        chunk_keys = SEL_BLOCK * SUBLANES
        nv = min(-(-(n_past + 1) * kt_sel // chunk_keys), ns // SUBLANES)
        chunks = [imp[c * SUBLANES:(c + 1) * SUBLANES] for c in range(nv)]
        ranks = [jnp.zeros((SUBLANES, tq), F32) for _ in range(nv)]
        sub8 = lax.broadcasted_iota(jnp.int32, (SUBLANES, 1), 0)
        for jp in range(nv * SUBLANES):
            cj, rj = divmod(jp, SUBLANES)
            other = chunks[cj][rj:rj + 1, :]
            for c in range(nv):
                if c > cj:
                    before_me = jnp.where(other >= chunks[c], 1.0, 0.0)
                elif c < cj:
                    before_me = jnp.where(other > chunks[c], 1.0, 0.0)
                else:
                    before_me = jnp.where(sub8 > rj, jnp.where(other >= chunks[c], 1.0, 0.0),
                                          jnp.where(other > chunks[c], 1.0, 0.0))
                ranks[c] = ranks[c] + before_me
        bias_rows = [jnp.where(valid[c * SUBLANES:(c + 1) * SUBLANES],
                               jnp.where(ranks[c] < topk, 0.0, SEL_MASK_BIAS), SEL_MASK_BIAS)
                     for c in range(nv)]
        if nv * SUBLANES < MAX_SEL_BLOCKS:
            bias_rows.append(jnp.full((MAX_SEL_BLOCKS - nv * SUBLANES, tq), SEL_MASK_BIAS, F32))
        bias_t = jnp.concatenate(bias_rows, axis=0)
        bias = jnp.concatenate([bias_t, bias_t], axis=0).T

        lo = lax.broadcasted_iota(jnp.int32, (tq, LANES), 1) < HEAD_DIM
        qa, qb = [], []
        for k in range(HEADS_PER_GROUP // 2):
            qs = q_ref[0, :, k * LANES:(k + 1) * LANES].astype(F32)
            qa.append(jnp.where(lo, qs, bias).astype(BF16))
            qb.append(jnp.where(lo, bias, qs).astype(BF16))
        q_forms = (jnp.concatenate(qa, axis=0), jnp.concatenate(qb, axis=0))

        kpos_w = w_keys.start + lax.broadcasted_iota(jnp.int32, (span, 1), 0)
        wbias = jnp.where(kpos_w <= t_lane, jnp.where(kpos_w > t_lane - WINDOW, 0.0, NEG_INF), NEG_INF)
        kpos_d = n_past * kt_sel + lax.broadcasted_iota(jnp.int32, (kt_sel, 1), 0)
        causal = jnp.where(kpos_d <= t_lane, 0.0, NEG_INF)
        sel_keys = lambda tile: slice(tile * kt_sel, (tile + 1) * kt_sel)
        jobs = [("win", k_win, vwt_ref, w_keys, None, twice(wbias)),
                ("sel", k_sel, vst_ref, sel_keys(n_past), q_forms, twice(causal))]
        jobs += [("sel", k_sel, vst_ref, sel_keys(tile), q_forms, None) for tile in range(n_past)]

        def scores(job, zeros):
            _, k_forms, _, keys, q, _ = job
            return [_dot_nt(k_forms[f][0, 0, keys, :], q[f] + zeros[f]) for f in range(2)]

        m, acc = {}, {}
        s_next = s_win
        for i, job in enumerate(jobs):
            branch, _, vt_ref, keys, _, tile_bias = job
            s_cur = s_next
            s, m_old, m_new = [None, None], [None, None], [None, None]
            for f in range(2):
                s[f] = s_cur[f] if tile_bias is None else s_cur[f] + tile_bias
                tile_max = jnp.max(s[f], axis=0, keepdims=True)
                m_old[f] = m.get((branch, f))
                m_new[f] = tile_max if m_old[f] is None else jnp.maximum(m_old[f], tile_max)
            if i + 1 < len(jobs):
                zeros = [(m_new[f][:, 0:LANES] * 0.0).astype(BF16) for f in range(2)]
                s_next = scores(jobs[i + 1], zeros)
            for f in range(2):
                pv = _dot(vt_ref[0, 0, :, keys], jnp.exp2(s[f] - m_new[f]).astype(BF16))
                if m_old[f] is None:
                    acc[branch, f] = pv
                else:
                    acc[branch, f] = jnp.exp2(m_old[f] - m_new[f]) * acc[branch, f] + pv
                m[branch, f] = m_new[f]
        o_win = [normalized(acc["win", f]) for f in range(2)]
        o_sel = [normalized(acc["sel", f]) for f in range(2)]

        gates_t = gate_ref[0].T
        for k in range(HEADS_PER_GROUP // 2):
            heads = []
            for r in (2 * k, 2 * k + 1):
                cols = slice((r // 2) * tq, (r // 2 + 1) * tq)
                branches = (o_cmp[r], o_sel[r % 2][:, cols], o_win[r % 2][:, cols])
                out = jnp.zeros((HEAD_DIM, tq), F32)
                for br in range(N_BRANCHES):
                    c = r * N_BRANCHES + br
                    out = out + gates_t[c:c + 1, :] * branches[br]
                heads.append(out)
            o_ref[0, :, k * LANES:(k + 1) * LANES] = jnp.concatenate(heads, axis=0).T

    for n_past in range(ksa_ref.shape[2] // kt_sel):
        pl.when(t0 // kt_sel == n_past)(functools.partial(step, n_past))


def _attention(q, gates, kcmp, vcmpt, ksa, ksb, vst, kwa, kwb, vwt):
    B, T, _ = q.shape
    nc = kcmp.shape[2]
    tq = min(Q_TILE, T)
    assert T % tq == 0 and T >= WINDOW + tq and T // SEL_BLOCK <= MAX_SEL_BLOCKS
    topk = min(SEL_TOPK, T // SEL_BLOCK)
    k_spec = pl.BlockSpec((1, 1, T, LANES), lambda b, g, t: (b, g, 0, 0))
    vt_spec = pl.BlockSpec((1, 1, VT_ROWS, T), lambda b, g, t: (b, g, 0, 0))
    return pl.pallas_call(
        functools.partial(_attn_kernel, topk=topk),
        grid=(B, N_KV_GROUPS, T // tq),
        in_specs=[pl.BlockSpec((1, tq, GROUP_WIDTH), lambda b, g, t: (b, t, g)),
                  pl.BlockSpec((1, tq, LANES), lambda b, g, t: (b, t, g)),
                  pl.BlockSpec((1, 2, nc, LANES), lambda b, g, t: (b, g, 0, 0)),
                  pl.BlockSpec((1, HEAD_DIM, nc), lambda b, g, t: (b, g, 0)),
                  k_spec, k_spec, vt_spec, k_spec, k_spec, vt_spec],
        out_specs=pl.BlockSpec((1, tq, GROUP_WIDTH), lambda b, g, t: (b, t, g)),
        out_shape=jax.ShapeDtypeStruct((B, T, ATTN_WIDTH), F32),
        scratch_shapes=[pltpu.VMEM((tq // LANES, nc, LANES), F32)],
        compiler_params=pltpu.CompilerParams(dimension_semantics=("arbitrary",) * 3,
                                             vmem_limit_bytes=VMEM_LIMIT),
        name="nsa_attention",
    )(q, gates, kcmp, vcmpt, ksa, ksb, vst, kwa, kwb, vwt)


def _ffn_kernel(x_ref, pool_ref, attn_ref, nattn_ref, wo_ref, n2_ref, wg_ref, wu_ref, cw_ref, cb_ref,
                wd_ref, nf_ref, o_ref, prev_ref, *, final):
    t = pl.program_id(1)
    tt = x_ref.shape[1]
    attn = _rms(attn_ref[0], nattn_ref[...]).astype(BF16)
    x1 = (x_ref[0] + _dot(pool_ref[0].astype(BF16), wo_ref[0:POOL_WIDTH, :])
          + _dot(attn, wo_ref[POOL_WIDTH:, :]))
    h = _rms(x1, n2_ref[...]).astype(BF16)

    @pl.when(t == 0)
    def _():
        prev_ref[...] = jnp.zeros(prev_ref.shape, F32)

    row = lax.broadcasted_iota(jnp.int32, (tt, 1), 0)
    o_ref[0] = x1
    for c0 in range(0, D_FF, FF_CHUNK):
        cols = slice(c0, min(c0 + FF_CHUNK, D_FF))
        g = _dot(h, wg_ref[:, cols])
        p1 = prev_ref[7:8, cols]
        p2 = prev_ref[6:7, cols]
        g1 = jnp.where(row == 0, p1, pltpu.roll(g, 1, axis=0))
        g2 = jnp.where(row == 0, p2, jnp.where(row == 1, p1, pltpu.roll(g, 2, axis=0)))
        prev_ref[:, cols] = g[tt - 8:tt, :]
        gc = (g * cw_ref[2:3, cols] + g1 * cw_ref[1:2, cols] + g2 * cw_ref[0:1, cols]
              + cb_ref[:, cols])
        act = (jax.nn.silu(gc) * _dot(h, wu_ref[:, cols])).astype(BF16)
        o_ref[0] += _dot(act, wd_ref[cols, :])
    if final:
        o_ref[0] = _rms(o_ref[0], nf_ref[...])


def _ffn(x, pool, attn, nattn, wo, n2, wg, wu, cw, cb, wd, nf, final):
    B, T, D = x.shape
    tt = min(ROW_TILE, T)
    row3 = lambda w: pl.BlockSpec((1, tt, w), lambda b, t: (b, t, 0))
    full = lambda a: pl.BlockSpec(a.shape, lambda b, t: (0,) * a.ndim, pipeline_mode=pl.Buffered(1))
    return pl.pallas_call(
        functools.partial(_ffn_kernel, final=final),
        grid=(B, T // tt),
        in_specs=[row3(D), row3(POOL_WIDTH), row3(ATTN_WIDTH), full(nattn), full(wo), full(n2), full(wg),
                  full(wu), full(cw), full(cb), full(wd), full(nf)],
        out_specs=row3(D),
        out_shape=jax.ShapeDtypeStruct((B, T, D), F32),
        scratch_shapes=[pltpu.VMEM((8, D_FF), F32)],
        compiler_params=pltpu.CompilerParams(dimension_semantics=("arbitrary", "arbitrary"),
                                             vmem_limit_bytes=VMEM_LIMIT),
        name="outproj_ffn",
    )(x, pool, attn, nattn, wo, n2, wg, wu, cw, cb, wd, nf)


def _compress_weights(pe, w1, w2, k_forms):
    half = CMP_LEN // 2
    eye = jnp.eye(N_KV_GROUPS, dtype=F32)
    pe_flat = jnp.broadcast_to(pe.reshape(2, half, 1, HEAD_DIM), (2, half, N_KV_GROUPS, HEAD_DIM))
    pe_flat = pe_flat.reshape(2, half * KV_WIDTH)
    w1r = w1.reshape(2, half, HEAD_DIM, CMP_HIDDEN)
    w1s = jnp.einsum("hldc,ge->hlgdec", w1r, eye).reshape(2, half * KV_WIDTH, N_KV_GROUPS * CMP_HIDDEN)
    if k_forms:
        place = jnp.zeros((2, HEAD_DIM, LANES), F32)
        place = place.at[0, :, :HEAD_DIM].set(jnp.eye(HEAD_DIM)).at[1, :, HEAD_DIM:].set(jnp.eye(HEAD_DIM))
        w2s = jnp.einsum("cd,ge,fdm->gcefm", w2, eye, place)
        w2s = w2s.reshape(N_KV_GROUPS * CMP_HIDDEN, N_KV_GROUPS * 2 * LANES)
    else:
        w2s = jnp.einsum("cd,ge->gced", w2, eye).reshape(N_KV_GROUPS * CMP_HIDDEN, KV_WIDTH)
    return pe_flat, w1s.astype(BF16), w2s.astype(BF16)


def kernel(x, norm1, w_in, w_pool, s_pool, cmp_pe_k, cmp_w1_k, cmp_w2_k, cmp_pe_v, cmp_w1_v, cmp_w2_v,
           norm_pool_out, norm_attn_out, w_out, norm2, w_gate, w_up, conv_w, conv_b, w_down, norm_f):
    B, T, D = x.shape
    depth = w_in.shape[0]
    nc = T // CMP_STRIDE
    row = lambda v: v.reshape(1, -1)
    n_gate = HEADS_PER_GROUP * N_BRANCHES
    for l in range(depth):
        w = w_in[l].astype(BF16)
        o_kv = POOL_WIDTH + ATTN_WIDTH
        o_g = o_kv + 6 * KV_WIDTH
        wg = jnp.zeros((D, N_KV_GROUPS * LANES), BF16)
        for g in range(N_KV_GROUPS):
            wg = wg.at[:, g * LANES:g * LANES + n_gate].set(w[:, o_g + g * n_gate:o_g + (g + 1) * n_gate])
        pool, q, kc, vc, gates, ksa, ksb, vst, kwa, kwb, vwt = _inproj(
            x, row(norm1[l]), w[:, :POOL_WIDTH], w[:, POOL_WIDTH:o_kv], w[:, o_kv:o_g], wg,
            w_pool[l].astype(BF16), row(s_pool[l]), row(norm_pool_out[l]))
        pek, w1k, w2k = _compress_weights(cmp_pe_k[l], cmp_w1_k[l], cmp_w2_k[l], True)
        pev, w1v, w2v = _compress_weights(cmp_pe_v[l], cmp_w1_v[l], cmp_w2_v[l], False)
        kcmp, vcmpt = _compress(kc.reshape(B, nc, CMP_STRIDE * KV_WIDTH), vc.reshape(B, nc, CMP_STRIDE * KV_WIDTH),
                                pek, pev, w1k, w1v, w2k, w2v)
        attn = _attention(q, gates, kcmp, vcmpt, ksa, ksb, vst, kwa, kwb, vwt)
        x = _ffn(x, pool, attn, row(norm_attn_out[l]), w_out[l].astype(BF16), row(norm2[l]),
                 w_gate[l].astype(BF16), w_up[l].astype(BF16), conv_w[l], row(conv_b[l]),
                 w_down[l].astype(BF16), row(norm_f), final=(l == depth - 1))
    return x
```

```python
import functools

import jax
import jax.numpy as jnp
from jax import lax
from jax.experimental import pallas as pl
from jax.experimental.pallas import tpu as pltpu

D_MODEL = 1024
POOL_WIDTH = 512
POOL_WINDOWS = (2, 4, 8, 16)
POOL_GROUP = 128
ATTN_WIDTH = 512
HEAD_DIM = 64
N_KV_GROUPS = 2
HEADS_PER_GROUP = 4
GROUP_WIDTH = HEADS_PER_GROUP * HEAD_DIM
KV_WIDTH = N_KV_GROUPS * HEAD_DIM
N_BRANCHES = 3
CMP_LEN = 32
CMP_STRIDE = 16
CMP_HIDDEN = 128
CMP_PER_SEL = 4
SEL_BLOCK = 64
SEL_TOPK = 16
WINDOW = 512
D_FF = 2816
EPS = 1e-6
NEG_INF = -1e30
FORCE_SCORE = 1e6
SCALE = HEAD_DIM ** -0.5
LOG2E = 1.4426950408889634

LANES = 128
SUBLANES = 8
BF16_ROWS = 16
MAX_SEL_BLOCKS = 64
SEL_MASK_BIAS = -(2.0 ** 100)
VT_ROWS = HEAD_DIM + BF16_ROWS
VMEM_LIMIT = 56 * 1024 * 1024

ROW_TILE = 512
PROJ_PARTS = 2
Q_TILE = 256
SEL_KEY_TILE = 512
HALO = 16
FF_CHUNK = 1024

F32 = jnp.float32
BF16 = jnp.bfloat16
NT_DIMS = (((1,), (1,)), ((), ()))


def _rms(x, g):
    return x * lax.rsqrt(jnp.mean(x * x, axis=-1, keepdims=True) + EPS) * g


def _dot(a, b):
    return jnp.dot(a, b, preferred_element_type=F32)


def _dot_nt(a, b):
    return lax.dot_general(a, b, NT_DIMS, preferred_element_type=F32)


def _inproj_kernel(x_ref, n1_ref, wu_ref, wq_ref, wkv_ref, wg_ref, wpool_ref, spool_ref, npool_ref,
                   pool_ref, q_ref, kc_ref, vc_ref, gate_ref,
                   ksa_ref, ksb_ref, vst_ref, kwa_ref, kwb_ref, vwt_ref,
                   ext_ref):
    t = pl.program_id(1)
    tt = x_ref.shape[1]

    @pl.when(t == 0)
    def _():
        ext_ref[0:HALO, :] = jnp.zeros((HALO, POOL_WIDTH), F32)

    @pl.when(t > 0)
    def _():
        ext_ref[0:HALO, :] = ext_ref[tt:tt + HALO, :]

    part = tt // PROJ_PARTS
    lane = lax.broadcasted_iota(jnp.int32, (part, LANES), 1)
    row = lax.broadcasted_iota(jnp.int32, (part, LANES), 0)
    lo = lane < HEAD_DIM
    for i in range(PROJ_PARTS):
        rows = slice(i * part, (i + 1) * part)
        h = _rms(x_ref[0, rows, :], n1_ref[...]).astype(BF16)
        q_ref[0, rows, :] = (_dot(h, wq_ref[...]) * (SCALE * LOG2E)).astype(BF16)
        gate_ref[0, rows, :] = jax.nn.sigmoid(_dot(h, wg_ref[...]))
        kv = _dot(h, wkv_ref[...])
        kc_ref[0, rows, :] = kv[:, 0:KV_WIDTH]
        vc_ref[0, rows, :] = kv[:, KV_WIDTH:2 * KV_WIDTH]
        ext_ref[HALO + i * part:HALO + (i + 1) * part, :] = _dot(h, wu_ref[...])

        blk = (t * tt + i * part + row) // SEL_BLOCK
        onehot_hi = jnp.where(lane - HEAD_DIM == blk, 1.0, 0.0)
        onehot_lo = jnp.where(lane == blk, 1.0, 0.0)

        def emit_k(pair, a_ref, b_ref, fill_hi, fill_lo):
            rolled = pltpu.roll(pair, HEAD_DIM, axis=1)
            a_ref[0, 0, rows, :] = jnp.where(lo, pair, fill_hi).astype(BF16)
            b_ref[0, 0, rows, :] = jnp.where(lo, fill_lo, rolled).astype(BF16)
            a_ref[0, 1, rows, :] = jnp.where(lo, rolled, fill_hi).astype(BF16)
            b_ref[0, 1, rows, :] = jnp.where(lo, fill_lo, pair).astype(BF16)

        def emit_vt(pair, vt_ref):
            pt = pair.T.astype(BF16)
            for g in range(N_KV_GROUPS):
                vt_ref[0, g, 0:HEAD_DIM, rows] = pt[g * HEAD_DIM:(g + 1) * HEAD_DIM]
                vt_ref[0, g, HEAD_DIM:VT_ROWS, rows] = jnp.ones((BF16_ROWS, part), BF16)

        emit_k(kv[:, 2 * KV_WIDTH:3 * KV_WIDTH], ksa_ref, ksb_ref, onehot_hi, onehot_lo)
        emit_vt(kv[:, 3 * KV_WIDTH:4 * KV_WIDTH], vst_ref)
        emit_k(kv[:, 4 * KV_WIDTH:5 * KV_WIDTH], kwa_ref, kwb_ref, 0.0, 0.0)
        emit_vt(kv[:, 5 * KV_WIDTH:6 * KV_WIDTH], vwt_ref)

    pos = t * tt + lax.broadcasted_iota(jnp.int32, (tt, 1), 0)
    mixed = []
    for p, w in enumerate(POOL_WINDOWS):
        cols = slice(p * POOL_GROUP, (p + 1) * POOL_GROUP)
        u = ext_ref[HALO:HALO + tt, cols]
        acc = u
        for k in range(1, w):
            acc = acc + ext_ref[HALO - k:HALO - k + tt, cols]
        cnt = jnp.minimum(pos + 1, w).astype(F32)
        pooled = acc / cnt - u
        mixed.append(_dot(pooled.astype(BF16), wpool_ref[p]))
    mixed = jnp.concatenate(mixed, axis=1) * spool_ref[...]
    pool_ref[0] = _rms(mixed, npool_ref[...])


def _inproj(x, n1, wu, wq, wkv, wg, wpool, spool, npool):
    B, T, D = x.shape
    tt = min(ROW_TILE, T)
    row3 = lambda w: pl.BlockSpec((1, tt, w), lambda b, t: (b, t, 0))
    k_spec = pl.BlockSpec((1, N_KV_GROUPS, tt, LANES), lambda b, t: (b, 0, t, 0))
    vt_spec = pl.BlockSpec((1, N_KV_GROUPS, VT_ROWS, tt), lambda b, t: (b, 0, 0, t))
    full = lambda a: pl.BlockSpec(a.shape, lambda b, t: (0,) * a.ndim)
    k_shape = jax.ShapeDtypeStruct((B, N_KV_GROUPS, T, LANES), BF16)
    vt_shape = jax.ShapeDtypeStruct((B, N_KV_GROUPS, VT_ROWS, T), BF16)
    return pl.pallas_call(
        _inproj_kernel,
        grid=(B, T // tt),
        in_specs=[row3(D), full(n1), full(wu), full(wq), full(wkv), full(wg), full(wpool), full(spool),
                  full(npool)],
        out_specs=[row3(POOL_WIDTH), row3(ATTN_WIDTH), row3(KV_WIDTH), row3(KV_WIDTH),
                   row3(N_KV_GROUPS * LANES), k_spec, k_spec, vt_spec, k_spec, k_spec, vt_spec],
        out_shape=[jax.ShapeDtypeStruct((B, T, POOL_WIDTH), F32),
                   jax.ShapeDtypeStruct((B, T, ATTN_WIDTH), BF16),
                   jax.ShapeDtypeStruct((B, T, KV_WIDTH), F32),
                   jax.ShapeDtypeStruct((B, T, KV_WIDTH), F32),
                   jax.ShapeDtypeStruct((B, T, N_KV_GROUPS * LANES), F32),
                   k_shape, k_shape, vt_shape, k_shape, k_shape, vt_shape],
        scratch_shapes=[pltpu.VMEM((HALO + tt, POOL_WIDTH), F32)],
        compiler_params=pltpu.CompilerParams(dimension_semantics=("arbitrary", "arbitrary"),
                                             vmem_limit_bytes=VMEM_LIMIT),
        name="inproj_pool",
    )(x, n1, wu, wq, wkv, wg, wpool, spool, npool)


def _compress_kernel(kc_ref, vc_ref, pek_ref, pev_ref, w1k_ref, w1v_ref, w2k_ref, w2v_ref,
                     kcmp_ref, vcmpt_ref):
    nc = kc_ref.shape[1]

    def hidden(c_ref, pe_ref, w1_ref):
        c = c_ref[0]
        top = _dot((c + pe_ref[0:1, :]).astype(BF16), w1_ref[0])
        bot = _dot((c + pe_ref[1:2, :]).astype(BF16), w1_ref[1])
        return jax.nn.gelu(top + pltpu.roll(bot, nc - 1, axis=0)).astype(BF16)

    kk = _dot(hidden(kc_ref, pek_ref, w1k_ref), w2k_ref[...])
    for i in range(2 * N_KV_GROUPS):
        kcmp_ref[0, i] = kk[:, i * LANES:(i + 1) * LANES].astype(BF16)
    vv = _dot(hidden(vc_ref, pev_ref, w1v_ref), w2v_ref[...])
    vcmpt_ref[0] = vv.T.astype(BF16)


def _compress(kc2, vc2, pek, pev, w1k, w1v, w2k, w2v):
    B, nc, width = kc2.shape
    full = lambda a: pl.BlockSpec(a.shape, lambda b: (0,) * a.ndim)
    chunk = pl.BlockSpec((1, nc, width), lambda b: (b, 0, 0))
    return pl.pallas_call(
        _compress_kernel,
        grid=(B,),
        in_specs=[chunk, chunk, full(pek), full(pev), full(w1k), full(w1v), full(w2k), full(w2v)],
        out_specs=[pl.BlockSpec((1, 2 * N_KV_GROUPS, nc, LANES), lambda b: (b, 0, 0, 0)),
                   pl.BlockSpec((1, KV_WIDTH, nc), lambda b: (b, 0, 0))],
        out_shape=[jax.ShapeDtypeStruct((B, 2 * N_KV_GROUPS, nc, LANES), BF16),
                   jax.ShapeDtypeStruct((B, KV_WIDTH, nc), BF16)],
        compiler_params=pltpu.CompilerParams(dimension_semantics=("arbitrary",),
                                             vmem_limit_bytes=VMEM_LIMIT),
        name="compress",
    )(kc2, vc2, pek, pev, w1k, w1v, w2k, w2v)


def _attn_kernel(q_ref, gate_ref, kc_ref, vct_ref, ksa_ref, ksb_ref, vst_ref, kwa_ref, kwb_ref, vwt_ref,
                 o_ref, psum_ref, bias_ref, owin_ref, osel_ref, *, topk):
    qt = pl.program_id(2)
    tq = q_ref.shape[1]
    nc = kc_ref.shape[2]
    ns = nc // CMP_PER_SEL
    t0 = qt * tq
    t_lane = t0 + lax.broadcasted_iota(jnp.int32, (1, tq), 1)

    def normalized(acc):
        return acc[0:HEAD_DIM] / acc[HEAD_DIM:HEAD_DIM + 1]

    q_pairs = jnp.concatenate([q_ref[0, :, k * LANES:(k + 1) * LANES] for k in range(HEADS_PER_GROUP // 2)],
                              axis=0)
    span = WINDOW + tq
    w_keys = pl.ds(pl.multiple_of(jnp.maximum(t0 - WINDOW, 0), tq), span)

    n_sub = lax.broadcasted_iota(jnp.int32, (nc, 1), 0)
    cbias = jnp.where(n_sub * CMP_STRIDE + (CMP_LEN - 1) <= t_lane, 0.0, NEG_INF)
    any_visible = jnp.where(t_lane >= CMP_LEN - 1, 1.0, 0.0)
    cbias2 = jnp.concatenate([cbias, cbias], axis=1)
    visible2 = jnp.concatenate([any_visible, any_visible], axis=1)
    psum = jnp.zeros((nc, tq), F32)
    o_cmp = [None] * HEADS_PER_GROUP
    s_cmp = [_dot_nt(kc_ref[0, f], q_pairs) for f in range(2)]
    for f in range(2):
        s = s_cmp[f] + cbias2
        e = jnp.exp2(s - jnp.max(s, axis=0, keepdims=True))
        p = e * (visible2 / jnp.sum(e, axis=0, keepdims=True))
        psum = psum + p[:, 0:tq] + p[:, tq:2 * tq]
        o = _dot(vct_ref[0], p.astype(BF16))
        o_cmp[f], o_cmp[f + 2] = o[:, 0:tq], o[:, tq:2 * tq]
    for i in range(tq // LANES):
        psum_ref[i] = psum[:, i * LANES:(i + 1) * LANES]

    j_sub = lax.broadcasted_iota(jnp.int32, (ns, 1), 0)
    parts = [jnp.concatenate([psum_ref[i, pl.ds(k, ns, stride=CMP_PER_SEL), :] for i in range(tq // LANES)],
                             axis=1) for k in range(CMP_PER_SEL)]
    before = jnp.where(j_sub == 0, 0.0, pltpu.roll(parts[CMP_PER_SEL - 1], 1, axis=0))
    imp = parts[0] + parts[1] + parts[2] + parts[3] + before
    cur = t_lane // SEL_BLOCK
    forced = (j_sub == 0) | (j_sub == cur) | (j_sub == cur - 1)
    valid = j_sub * SEL_BLOCK <= t_lane
    imp = jnp.where(forced, FORCE_SCORE, imp)
    imp = jnp.where(valid, imp, NEG_INF)

    n_chunks = ns // SUBLANES
    chunks = [imp[c * SUBLANES:(c + 1) * SUBLANES] for c in range(n_chunks)]
    sub8 = lax.broadcasted_iota(jnp.int32, (SUBLANES, 1), 0)
    bias_ref[...] = jnp.full(bias_ref.shape, SEL_MASK_BIAS, F32)
    causal_chunks = (t0 + tq - 1) // (SEL_BLOCK * SUBLANES) + 1
    for nv in range(1, n_chunks + 1):
        @pl.when(causal_chunks == nv)
        def _():
            ranks = [jnp.zeros((SUBLANES, tq), F32) for _ in range(nv)]
            for jp in range(nv * SUBLANES):
                cj, rj = divmod(jp, SUBLANES)
                other = chunks[cj][rj:rj + 1, :]
                for c in range(nv):
                    if c > cj:
                        before_me = jnp.where(other >= chunks[c], 1.0, 0.0)
                    elif c < cj:
                        before_me = jnp.where(other > chunks[c], 1.0, 0.0)
                    else:
                        before_me = jnp.where(sub8 > rj, jnp.where(other >= chunks[c], 1.0, 0.0),
                                              jnp.where(other > chunks[c], 1.0, 0.0))
                    ranks[c] = ranks[c] + before_me
            for c in range(nv):
                rows = slice(c * SUBLANES, (c + 1) * SUBLANES)
                bias_ref[rows, :] = jnp.where(valid[rows], jnp.where(ranks[c] < topk, 0.0, SEL_MASK_BIAS),
                                              SEL_MASK_BIAS)
    bias_t = bias_ref[...]
    bias = jnp.concatenate([bias_t, bias_t], axis=0).T

    lane = lax.broadcasted_iota(jnp.int32, (tq, LANES), 1)
    lo = lane < HEAD_DIM
    qa, qb = [], []
    for k in range(HEADS_PER_GROUP // 2):
        qs = q_ref[0, :, k * LANES:(k + 1) * LANES].astype(F32)
        qa.append(jnp.where(lo, qs, bias).astype(BF16))
        qb.append(jnp.where(lo, bias, qs).astype(BF16))
    q_forms = (jnp.concatenate(qa, axis=0), jnp.concatenate(qb, axis=0))
    k_sel = (ksa_ref, ksb_ref)
    kt_sel = min(SEL_KEY_TILE, ksa_ref.shape[2])

    def window_and_selected(n_past):
        kpos_w = w_keys.start + lax.broadcasted_iota(jnp.int32, (span, 1), 0)
        wbias = jnp.where(kpos_w <= t_lane, jnp.where(kpos_w > t_lane - WINDOW, 0.0, NEG_INF), NEG_INF)
        kpos_d = n_past * kt_sel + lax.broadcasted_iota(jnp.int32, (kt_sel, 1), 0)
        causal = jnp.where(kpos_d <= t_lane, 0.0, NEG_INF)
        twice = lambda a: jnp.concatenate([a, a], axis=1)
        sel_keys = lambda tile: slice(tile * kt_sel, (tile + 1) * kt_sel)
        jobs = [("win", (kwa_ref, kwb_ref), vwt_ref, w_keys, (q_pairs, q_pairs), twice(wbias)),
                ("sel", k_sel, vst_ref, sel_keys(n_past), q_forms, twice(causal))]
        jobs += [("sel", k_sel, vst_ref, sel_keys(tile), q_forms, None) for tile in range(n_past)]

        def scores(job, zeros):
            _, k_forms, _, keys, q, _ = job
            return [_dot_nt(k_forms[f][0, 0, keys, :], q[f] if zeros is None else q[f] + zeros[f])
                    for f in range(2)]

        m, acc = {}, {}
        s_next = scores(jobs[0], None)
        for i, job in enumerate(jobs):
            branch, _, vt_ref, keys, _, bias = job
            s_cur = s_next
            s, m_old, m_new = [None, None], [None, None], [None, None]
            for f in range(2):
                s[f] = s_cur[f] if bias is None else s_cur[f] + bias
                tile_max = jnp.max(s[f], axis=0, keepdims=True)
                m_old[f] = m.get((branch, f))
                m_new[f] = tile_max if m_old[f] is None else jnp.maximum(m_old[f], tile_max)
            if i + 1 < len(jobs):
                zeros = [(m_new[f][:, 0:LANES] * 0.0).astype(BF16) for f in range(2)]
                s_next = scores(jobs[i + 1], zeros)
            for f in range(2):
                pv = _dot(vt_ref[0, 0, :, keys], jnp.exp2(s[f] - m_new[f]).astype(BF16))
                if m_old[f] is None:
                    acc[branch, f] = pv
                else:
                    acc[branch, f] = jnp.exp2(m_old[f] - m_new[f]) * acc[branch, f] + pv
                m[branch, f] = m_new[f]
        return acc

    for n_past in range(ksa_ref.shape[2] // kt_sel):
        @pl.when(t0 // kt_sel == n_past)
        def _():
            acc = window_and_selected(n_past)
            for f in range(2):
                owin_ref[f] = normalized(acc["win", f])
                osel_ref[f] = normalized(acc["sel", f])
    o_win = [owin_ref[f] for f in range(2)]
    o_sel = [osel_ref[f] for f in range(2)]

    gates_t = gate_ref[0].T
    for k in range(HEADS_PER_GROUP // 2):
        heads = []
        for r in (2 * k, 2 * k + 1):
            cols = slice((r // 2) * tq, (r // 2 + 1) * tq)
            branches = (o_cmp[r], o_sel[r % 2][:, cols], o_win[r % 2][:, cols])
            out = jnp.zeros((HEAD_DIM, tq), F32)
            for br in range(N_BRANCHES):
                c = r * N_BRANCHES + br
                out = out + gates_t[c:c + 1, :] * branches[br]
            heads.append(out)
        o_ref[0, :, k * LANES:(k + 1) * LANES] = jnp.concatenate(heads, axis=0).T


def _attention(q, gates, kcmp, vcmpt, ksa, ksb, vst, kwa, kwb, vwt):
    B, T, _ = q.shape
    nc = kcmp.shape[2]
    tq = min(Q_TILE, T)
    assert T % tq == 0 and T >= WINDOW + tq and T // SEL_BLOCK <= MAX_SEL_BLOCKS
    topk = min(SEL_TOPK, T // SEL_BLOCK)
    k_spec = pl.BlockSpec((1, 1, T, LANES), lambda b, g, t: (b, g, 0, 0))
    vt_spec = pl.BlockSpec((1, 1, VT_ROWS, T), lambda b, g, t: (b, g, 0, 0))
    return pl.pallas_call(
        functools.partial(_attn_kernel, topk=topk),
        grid=(B, N_KV_GROUPS, T // tq),
        in_specs=[pl.BlockSpec((1, tq, GROUP_WIDTH), lambda b, g, t: (b, t, g)),
                  pl.BlockSpec((1, tq, LANES), lambda b, g, t: (b, t, g)),
                  pl.BlockSpec((1, 2, nc, LANES), lambda b, g, t: (b, g, 0, 0)),
                  pl.BlockSpec((1, HEAD_DIM, nc), lambda b, g, t: (b, g, 0)),
                  k_spec, k_spec, vt_spec, k_spec, k_spec, vt_spec],
        out_specs=pl.BlockSpec((1, tq, GROUP_WIDTH), lambda b, g, t: (b, t, g)),
        out_shape=jax.ShapeDtypeStruct((B, T, ATTN_WIDTH), F32),
        scratch_shapes=[pltpu.VMEM((tq // LANES, nc, LANES), F32),
                        pltpu.VMEM((MAX_SEL_BLOCKS, tq), F32),
                        pltpu.VMEM((2, HEAD_DIM, 2 * tq), F32),
                        pltpu.VMEM((2, HEAD_DIM, 2 * tq), F32)],
        compiler_params=pltpu.CompilerParams(dimension_semantics=("arbitrary",) * 3,
                                             vmem_limit_bytes=VMEM_LIMIT),
        name="nsa_attention",
    )(q, gates, kcmp, vcmpt, ksa, ksb, vst, kwa, kwb, vwt)


def _ffn_kernel(x_ref, pool_ref, attn_ref, nattn_ref, wo_ref, n2_ref, wg_ref, wu_ref, cw_ref, cb_ref,
                wd_ref, nf_ref, o_ref, prev_ref, *, final):
    t = pl.program_id(1)
    tt = x_ref.shape[1]
    attn = _rms(attn_ref[0], nattn_ref[...]).astype(BF16)
    x1 = (x_ref[0] + _dot(pool_ref[0].astype(BF16), wo_ref[0:POOL_WIDTH, :])
          + _dot(attn, wo_ref[POOL_WIDTH:, :]))
    h = _rms(x1, n2_ref[...]).astype(BF16)

    @pl.when(t == 0)
    def _():
        prev_ref[...] = jnp.zeros(prev_ref.shape, F32)

    row = lax.broadcasted_iota(jnp.int32, (tt, 1), 0)
    o_ref[0] = x1
    for c0 in range(0, D_FF, FF_CHUNK):
        cols = slice(c0, min(c0 + FF_CHUNK, D_FF))
        g = _dot(h, wg_ref[:, cols])
        p1 = prev_ref[7:8, cols]
        p2 = prev_ref[6:7, cols]
        g1 = jnp.where(row == 0, p1, pltpu.roll(g, 1, axis=0))
        g2 = jnp.where(row == 0, p2, jnp.where(row == 1, p1, pltpu.roll(g, 2, axis=0)))
        prev_ref[:, cols] = g[tt - 8:tt, :]
        gc = (g * cw_ref[2:3, cols] + g1 * cw_ref[1:2, cols] + g2 * cw_ref[0:1, cols]
              + cb_ref[:, cols])
        act = (jax.nn.silu(gc) * _dot(h, wu_ref[:, cols])).astype(BF16)
        o_ref[0] += _dot(act, wd_ref[cols, :])
    if final:
        o_ref[0] = _rms(o_ref[0], nf_ref[...])


def _ffn(x, pool, attn, nattn, wo, n2, wg, wu, cw, cb, wd, nf, final):
    B, T, D = x.shape
    tt = min(ROW_TILE, T)
    row3 = lambda w: pl.BlockSpec((1, tt, w), lambda b, t: (b, t, 0))
    full = lambda a: pl.BlockSpec(a.shape, lambda b, t: (0,) * a.ndim, pipeline_mode=pl.Buffered(1))
    return pl.pallas_call(
        functools.partial(_ffn_kernel, final=final),
        grid=(B, T // tt),
        in_specs=[row3(D), row3(POOL_WIDTH), row3(ATTN_WIDTH), full(nattn), full(wo), full(n2), full(wg),
                  full(wu), full(cw), full(cb), full(wd), full(nf)],
        out_specs=row3(D),
        out_shape=jax.ShapeDtypeStruct((B, T, D), F32),
        scratch_shapes=[pltpu.VMEM((8, D_FF), F32)],
        compiler_params=pltpu.CompilerParams(dimension_semantics=("arbitrary", "arbitrary"),
                                             vmem_limit_bytes=VMEM_LIMIT),
        name="outproj_ffn",
    )(x, pool, attn, nattn, wo, n2, wg, wu, cw, cb, wd, nf)


def _compress_weights(pe, w1, w2, k_forms):
    half = CMP_LEN // 2
    eye = jnp.eye(N_KV_GROUPS, dtype=F32)
    pe_flat = jnp.broadcast_to(pe.reshape(2, half, 1, HEAD_DIM), (2, half, N_KV_GROUPS, HEAD_DIM))
    pe_flat = pe_flat.reshape(2, half * KV_WIDTH)
    w1r = w1.reshape(2, half, HEAD_DIM, CMP_HIDDEN)
    w1s = jnp.einsum("hldc,ge->hlgdec", w1r, eye).reshape(2, half * KV_WIDTH, N_KV_GROUPS * CMP_HIDDEN)
    if k_forms:
        place = jnp.zeros((2, HEAD_DIM, LANES), F32)
        place = place.at[0, :, :HEAD_DIM].set(jnp.eye(HEAD_DIM)).at[1, :, HEAD_DIM:].set(jnp.eye(HEAD_DIM))
        w2s = jnp.einsum("cd,ge,fdm->gcefm", w2, eye, place)
        w2s = w2s.reshape(N_KV_GROUPS * CMP_HIDDEN, N_KV_GROUPS * 2 * LANES)
    else:
        w2s = jnp.einsum("cd,ge->gced", w2, eye).reshape(N_KV_GROUPS * CMP_HIDDEN, KV_WIDTH)
    return pe_flat, w1s.astype(BF16), w2s.astype(BF16)


def kernel(x, norm1, w_in, w_pool, s_pool, cmp_pe_k, cmp_w1_k, cmp_w2_k, cmp_pe_v, cmp_w1_v, cmp_w2_v,
           norm_pool_out, norm_attn_out, w_out, norm2, w_gate, w_up, conv_w, conv_b, w_down, norm_f):
    B, T, D = x.shape
    depth = w_in.shape[0]
    nc = T // CMP_STRIDE
    row = lambda v: v.reshape(1, -1)
    n_gate = HEADS_PER_GROUP * N_BRANCHES
    for l in range(depth):
        w = w_in[l].astype(BF16)
        o_kv = POOL_WIDTH + ATTN_WIDTH
        o_g = o_kv + 6 * KV_WIDTH
        wg = jnp.zeros((D, N_KV_GROUPS * LANES), BF16)
        for g in range(N_KV_GROUPS):
            wg = wg.at[:, g * LANES:g * LANES + n_gate].set(w[:, o_g + g * n_gate:o_g + (g + 1) * n_gate])
        pool, q, kc, vc, gates, ksa, ksb, vst, kwa, kwb, vwt = _inproj(
            x, row(norm1[l]), w[:, :POOL_WIDTH], w[:, POOL_WIDTH:o_kv], w[:, o_kv:o_g], wg,
            w_pool[l].astype(BF16), row(s_pool[l]), row(norm_pool_out[l]))
        pek, w1k, w2k = _compress_weights(cmp_pe_k[l], cmp_w1_k[l], cmp_w2_k[l], True)
        pev, w1v, w2v = _compress_weights(cmp_pe_v[l], cmp_w1_v[l], cmp_w2_v[l], False)
        kcmp, vcmpt = _compress(kc.reshape(B, nc, CMP_STRIDE * KV_WIDTH), vc.reshape(B, nc, CMP_STRIDE * KV_WIDTH),
                                pek, pev, w1k, w1v, w2k, w2v)
        attn = _attention(q, gates, kcmp, vcmpt, ksa, ksb, vst, kwa, kwb, vwt)
        x = _ffn(x, pool, attn, row(norm_attn_out[l]), w_out[l].astype(BF16), row(norm2[l]),
                 w_gate[l].astype(BF16), w_up[l].astype(BF16), conv_w[l], row(conv_b[l]),
                 w_down[l].astype(BF16), row(norm_f), final=(l == depth - 1))
    return x
```

```python
import functools

import jax
import jax.numpy as jnp
from jax import lax
from jax.experimental import pallas as pl
from jax.experimental.pallas import tpu as pltpu

D_MODEL = 1024
POOL_WIDTH = 512
POOL_WINDOWS = (2, 4, 8, 16)
POOL_GROUP = 128
ATTN_WIDTH = 512
HEAD_DIM = 64
N_KV_GROUPS = 2
HEADS_PER_GROUP = 4
GROUP_WIDTH = HEADS_PER_GROUP * HEAD_DIM
KV_WIDTH = N_KV_GROUPS * HEAD_DIM
N_BRANCHES = 3
CMP_LEN = 32
CMP_STRIDE = 16
CMP_HIDDEN = 128
CMP_PER_SEL = 4
SEL_BLOCK = 64
SEL_TOPK = 16
WINDOW = 512
D_FF = 2816
EPS = 1e-6
NEG_INF = -1e30
FORCE_SCORE = 1e6
SCALE = HEAD_DIM ** -0.5
LOG2E = 1.4426950408889634

LANES = 128
SUBLANES = 8
BF16_ROWS = 16
MAX_SEL_BLOCKS = 64
SEL_MASK_BIAS = -(2.0 ** 100)
VT_ROWS = HEAD_DIM + BF16_ROWS
VMEM_LIMIT = 56 * 1024 * 1024

ROW_TILE = 512
PROJ_PARTS = 2
Q_TILE = 256
SEL_KEY_TILE = 512
HALO = 16
FF_CHUNK = 1024

F32 = jnp.float32
BF16 = jnp.bfloat16
NT_DIMS = (((1,), (1,)), ((), ()))


def _rms(x, g):
    return x * lax.rsqrt(jnp.mean(x * x, axis=-1, keepdims=True) + EPS) * g


def _dot(a, b):
    return jnp.dot(a, b, preferred_element_type=F32)


def _dot_nt(a, b):
    return lax.dot_general(a, b, NT_DIMS, preferred_element_type=F32)


def _inproj_kernel(x_ref, n1_ref, wu_ref, wq_ref, wkv_ref, wg_ref, wpool_ref, spool_ref, npool_ref,
                   pool_ref, q_ref, kc_ref, vc_ref, gate_ref,
                   ksa_ref, ksb_ref, vst_ref, kwa_ref, kwb_ref, vwt_ref,
                   ext_ref):
    t = pl.program_id(1)
    tt = x_ref.shape[1]

    @pl.when(t == 0)
    def _():
        ext_ref[0:HALO, :] = jnp.zeros((HALO, POOL_WIDTH), F32)

    @pl.when(t > 0)
    def _():
        ext_ref[0:HALO, :] = ext_ref[tt:tt + HALO, :]

    part = tt // PROJ_PARTS
    lane = lax.broadcasted_iota(jnp.int32, (part, LANES), 1)
    row = lax.broadcasted_iota(jnp.int32, (part, LANES), 0)
    lo = lane < HEAD_DIM
    for i in range(PROJ_PARTS):
        rows = slice(i * part, (i + 1) * part)
        h = _rms(x_ref[0, rows, :], n1_ref[...]).astype(BF16)
        q_ref[0, rows, :] = (_dot(h, wq_ref[...]) * (SCALE * LOG2E)).astype(BF16)
        gate_ref[0, rows, :] = jax.nn.sigmoid(_dot(h, wg_ref[...]))
        kv = _dot(h, wkv_ref[...])
        kc_ref[0, rows, :] = kv[:, 0:KV_WIDTH]
        vc_ref[0, rows, :] = kv[:, KV_WIDTH:2 * KV_WIDTH]
        ext_ref[HALO + i * part:HALO + (i + 1) * part, :] = _dot(h, wu_ref[...])

        blk = (t * tt + i * part + row) // SEL_BLOCK
        onehot_hi = jnp.where(lane - HEAD_DIM == blk, 1.0, 0.0)
        onehot_lo = jnp.where(lane == blk, 1.0, 0.0)

        def emit_k(pair, a_ref, b_ref, fill_hi, fill_lo):
            rolled = pltpu.roll(pair, HEAD_DIM, axis=1)
            a_ref[0, 0, rows, :] = jnp.where(lo, pair, fill_hi).astype(BF16)
            b_ref[0, 0, rows, :] = jnp.where(lo, fill_lo, rolled).astype(BF16)
            a_ref[0, 1, rows, :] = jnp.where(lo, rolled, fill_hi).astype(BF16)
            b_ref[0, 1, rows, :] = jnp.where(lo, fill_lo, pair).astype(BF16)

        def emit_vt(pair, vt_ref):
            pt = pair.T.astype(BF16)
            for g in range(N_KV_GROUPS):
                vt_ref[0, g, 0:HEAD_DIM, rows] = pt[g * HEAD_DIM:(g + 1) * HEAD_DIM]
                vt_ref[0, g, HEAD_DIM:VT_ROWS, rows] = jnp.ones((BF16_ROWS, part), BF16)

        emit_k(kv[:, 2 * KV_WIDTH:3 * KV_WIDTH], ksa_ref, ksb_ref, onehot_hi, onehot_lo)
        emit_vt(kv[:, 3 * KV_WIDTH:4 * KV_WIDTH], vst_ref)
        emit_k(kv[:, 4 * KV_WIDTH:5 * KV_WIDTH], kwa_ref, kwb_ref, 0.0, 0.0)
        emit_vt(kv[:, 5 * KV_WIDTH:6 * KV_WIDTH], vwt_ref)

    pos = t * tt + lax.broadcasted_iota(jnp.int32, (tt, 1), 0)
    mixed = []
    for p, w in enumerate(POOL_WINDOWS):
        cols = slice(p * POOL_GROUP, (p + 1) * POOL_GROUP)
        sums = ext_ref[:, cols]
        shift = 1
        while shift < w:
            sums = sums + pltpu.roll(sums, shift, axis=0)
            shift *= 2
        u = ext_ref[HALO:HALO + tt, cols]
        acc = sums[HALO:HALO + tt]
        cnt = jnp.minimum(pos + 1, w).astype(F32)
        pooled = acc / cnt - u
        mixed.append(_dot(pooled.astype(BF16), wpool_ref[p]))
    mixed = jnp.concatenate(mixed, axis=1) * spool_ref[...]
    pool_ref[0] = _rms(mixed, npool_ref[...])


def _inproj(x, n1, wu, wq, wkv, wg, wpool, spool, npool):
    B, T, D = x.shape
    tt = min(ROW_TILE, T)
    row3 = lambda w: pl.BlockSpec((1, tt, w), lambda b, t: (b, t, 0))
    k_spec = pl.BlockSpec((1, N_KV_GROUPS, tt, LANES), lambda b, t: (b, 0, t, 0))
    vt_spec = pl.BlockSpec((1, N_KV_GROUPS, VT_ROWS, tt), lambda b, t: (b, 0, 0, t))
    full = lambda a: pl.BlockSpec(a.shape, lambda b, t: (0,) * a.ndim)
    k_shape = jax.ShapeDtypeStruct((B, N_KV_GROUPS, T, LANES), BF16)
    vt_shape = jax.ShapeDtypeStruct((B, N_KV_GROUPS, VT_ROWS, T), BF16)
    return pl.pallas_call(
        _inproj_kernel,
        grid=(B, T // tt),
        in_specs=[row3(D), full(n1), full(wu), full(wq), full(wkv), full(wg), full(wpool), full(spool),
                  full(npool)],
        out_specs=[row3(POOL_WIDTH), row3(ATTN_WIDTH), row3(KV_WIDTH), row3(KV_WIDTH),
                   row3(N_KV_GROUPS * LANES), k_spec, k_spec, vt_spec, k_spec, k_spec, vt_spec],
        out_shape=[jax.ShapeDtypeStruct((B, T, POOL_WIDTH), F32),
                   jax.ShapeDtypeStruct((B, T, ATTN_WIDTH), BF16),
                   jax.ShapeDtypeStruct((B, T, KV_WIDTH), F32),
                   jax.ShapeDtypeStruct((B, T, KV_WIDTH), F32),
                   jax.ShapeDtypeStruct((B, T, N_KV_GROUPS * LANES), F32),
                   k_shape, k_shape, vt_shape, k_shape, k_shape, vt_shape],
        scratch_shapes=[pltpu.VMEM((HALO + tt, POOL_WIDTH), F32)],
        compiler_params=pltpu.CompilerParams(dimension_semantics=("arbitrary", "arbitrary"),
                                             vmem_limit_bytes=VMEM_LIMIT),
        name="inproj_pool",
    )(x, n1, wu, wq, wkv, wg, wpool, spool, npool)


def _compress_kernel(kc_ref, vc_ref, pek_ref, pev_ref, w1k_ref, w1v_ref, w2k_ref, w2v_ref,
                     kcmp_ref, vcmpt_ref):
    nc = kc_ref.shape[1]

    def hidden(c_ref, pe_ref, w1_ref):
        c = c_ref[0]
        top = _dot((c + pe_ref[0:1, :]).astype(BF16), w1_ref[0])
        bot = _dot((c + pe_ref[1:2, :]).astype(BF16), w1_ref[1])
        return jax.nn.gelu(top + pltpu.roll(bot, nc - 1, axis=0)).astype(BF16)

    kk = _dot(hidden(kc_ref, pek_ref, w1k_ref), w2k_ref[...])
    for i in range(2 * N_KV_GROUPS):
        kcmp_ref[0, i] = kk[:, i * LANES:(i + 1) * LANES].astype(BF16)
    vv = _dot(hidden(vc_ref, pev_ref, w1v_ref), w2v_ref[...])
    vcmpt_ref[0] = vv.T.astype(BF16)


def _compress(kc2, vc2, pek, pev, w1k, w1v, w2k, w2v):
    B, nc, width = kc2.shape
    full = lambda a: pl.BlockSpec(a.shape, lambda b: (0,) * a.ndim)
    chunk = pl.BlockSpec((1, nc, width), lambda b: (b, 0, 0))
    return pl.pallas_call(
        _compress_kernel,
        grid=(B,),
        in_specs=[chunk, chunk, full(pek), full(pev), full(w1k), full(w1v), full(w2k), full(w2v)],
        out_specs=[pl.BlockSpec((1, 2 * N_KV_GROUPS, nc, LANES), lambda b: (b, 0, 0, 0)),
                   pl.BlockSpec((1, KV_WIDTH, nc), lambda b: (b, 0, 0))],
        out_shape=[jax.ShapeDtypeStruct((B, 2 * N_KV_GROUPS, nc, LANES), BF16),
                   jax.ShapeDtypeStruct((B, KV_WIDTH, nc), BF16)],
        compiler_params=pltpu.CompilerParams(dimension_semantics=("arbitrary",),
                                             vmem_limit_bytes=VMEM_LIMIT),
        name="compress",
    )(kc2, vc2, pek, pev, w1k, w1v, w2k, w2v)


def _attn_kernel(q_ref, gate_ref, kc_ref, vct_ref, ksa_ref, ksb_ref, vst_ref, kwa_ref, kwb_ref, vwt_ref,
                 o_ref, psum_ref, bias_ref, owin_ref, osel_ref, *, topk):
    qt = pl.program_id(2)
    tq = q_ref.shape[1]
    nc = kc_ref.shape[2]
    ns = nc // CMP_PER_SEL
    t0 = qt * tq
    t_lane = t0 + lax.broadcasted_iota(jnp.int32, (1, tq), 1)

    def normalized(acc):
        return acc[0:HEAD_DIM] / acc[HEAD_DIM:HEAD_DIM + 1]

    q_pairs = jnp.concatenate([q_ref[0, :, k * LANES:(k + 1) * LANES] for k in range(HEADS_PER_GROUP // 2)],
                              axis=0)
    span = WINDOW + tq
    w_keys = pl.ds(pl.multiple_of(jnp.maximum(t0 - WINDOW, 0), tq), span)

    n_sub = lax.broadcasted_iota(jnp.int32, (nc, 1), 0)
    cbias = jnp.where(n_sub * CMP_STRIDE + (CMP_LEN - 1) <= t_lane, 0.0, NEG_INF)
    any_visible = jnp.where(t_lane >= CMP_LEN - 1, 1.0, 0.0)
    cbias2 = jnp.concatenate([cbias, cbias], axis=1)
    visible2 = jnp.concatenate([any_visible, any_visible], axis=1)
    psum = jnp.zeros((nc, tq), F32)
    o_cmp = [None] * HEADS_PER_GROUP
    s_cmp = [_dot_nt(kc_ref[0, f], q_pairs) for f in range(2)]
    for f in range(2):
        s = s_cmp[f] + cbias2
        e = jnp.exp2(s - jnp.max(s, axis=0, keepdims=True))
        p = e * (visible2 / jnp.sum(e, axis=0, keepdims=True))
        psum = psum + p[:, 0:tq] + p[:, tq:2 * tq]
        o = _dot(vct_ref[0], p.astype(BF16))
        o_cmp[f], o_cmp[f + 2] = o[:, 0:tq], o[:, tq:2 * tq]
    for i in range(tq // LANES):
        psum_ref[i] = psum[:, i * LANES:(i + 1) * LANES]

    j_sub = lax.broadcasted_iota(jnp.int32, (ns, 1), 0)
    parts = [jnp.concatenate([psum_ref[i, pl.ds(k, ns, stride=CMP_PER_SEL), :] for i in range(tq // LANES)],
                             axis=1) for k in range(CMP_PER_SEL)]
    before = jnp.where(j_sub == 0, 0.0, pltpu.roll(parts[CMP_PER_SEL - 1], 1, axis=0))
    imp = parts[0] + parts[1] + parts[2] + parts[3] + before
    cur = t_lane // SEL_BLOCK
    forced = (j_sub == 0) | (j_sub == cur) | (j_sub == cur - 1)
    valid = j_sub * SEL_BLOCK <= t_lane
    imp = jnp.where(forced, FORCE_SCORE, imp)
    imp = jnp.where(valid, imp, NEG_INF)

    n_chunks = ns // SUBLANES
    chunks = [imp[c * SUBLANES:(c + 1) * SUBLANES] for c in range(n_chunks)]
    sub8 = lax.broadcasted_iota(jnp.int32, (SUBLANES, 1), 0)
    bias_ref[...] = jnp.full(bias_ref.shape, SEL_MASK_BIAS, F32)
    causal_chunks = (t0 + tq - 1) // (SEL_BLOCK * SUBLANES) + 1
    for nv in range(1, n_chunks + 1):
        @pl.when(causal_chunks == nv)
        def _():
            ranks = [jnp.zeros((SUBLANES, tq), F32) for _ in range(nv)]
            for jp in range(nv * SUBLANES):
                cj, rj = divmod(jp, SUBLANES)
                other = chunks[cj][rj:rj + 1, :]
                for c in range(nv):
                    if c > cj:
                        before_me = jnp.where(other >= chunks[c], 1.0, 0.0)
                    elif c < cj:
                        before_me = jnp.where(other > chunks[c], 1.0, 0.0)
                    else:
                        before_me = jnp.where(sub8 > rj, jnp.where(other >= chunks[c], 1.0, 0.0),
                                              jnp.where(other > chunks[c], 1.0, 0.0))
                    ranks[c] = ranks[c] + before_me
            for c in range(nv):
                rows = slice(c * SUBLANES, (c + 1) * SUBLANES)
                bias_ref[rows, :] = jnp.where(valid[rows], jnp.where(ranks[c] < topk, 0.0, SEL_MASK_BIAS),
                                              SEL_MASK_BIAS)
    bias_t = bias_ref[...]
    bias = jnp.concatenate([bias_t, bias_t], axis=0).T

    lane = lax.broadcasted_iota(jnp.int32, (tq, LANES), 1)
    lo = lane < HEAD_DIM
    qa, qb = [], []
    for k in range(HEADS_PER_GROUP // 2):
        qs = q_ref[0, :, k * LANES:(k + 1) * LANES].astype(F32)
        qa.append(jnp.where(lo, qs, bias).astype(BF16))
        qb.append(jnp.where(lo, bias, qs).astype(BF16))
    q_forms = (jnp.concatenate(qa, axis=0), jnp.concatenate(qb, axis=0))
    k_sel = (ksa_ref, ksb_ref)
    kt_sel = min(SEL_KEY_TILE, ksa_ref.shape[2])

    def window_and_selected(n_past):
        kpos_w = w_keys.start + lax.broadcasted_iota(jnp.int32, (span, 1), 0)
        wbias = jnp.where(kpos_w <= t_lane, jnp.where(kpos_w > t_lane - WINDOW, 0.0, NEG_INF), NEG_INF)
        kpos_d = n_past * kt_sel + lax.broadcasted_iota(jnp.int32, (kt_sel, 1), 0)
        causal = jnp.where(kpos_d <= t_lane, 0.0, NEG_INF)
        twice = lambda a: jnp.concatenate([a, a], axis=1)
        sel_keys = lambda tile: slice(tile * kt_sel, (tile + 1) * kt_sel)
        jobs = [("win", (kwa_ref, kwb_ref), vwt_ref, w_keys, (q_pairs, q_pairs), twice(wbias)),
                ("sel", k_sel, vst_ref, sel_keys(n_past), q_forms, twice(causal))]
        jobs += [("sel", k_sel, vst_ref, sel_keys(tile), q_forms, None) for tile in range(n_past)]

        def scores(job, zeros):
            _, k_forms, _, keys, q, _ = job
            return [_dot_nt(k_forms[f][0, 0, keys, :], q[f] if zeros is None else q[f] + zeros[f])
                    for f in range(2)]

        m, acc = {}, {}
        s_next = scores(jobs[0], None)
        for i, job in enumerate(jobs):
            branch, _, vt_ref, keys, _, bias = job
            s_cur = s_next
            s, m_old, m_new = [None, None], [None, None], [None, None]
            for f in range(2):
                s[f] = s_cur[f] if bias is None else s_cur[f] + bias
                tile_max = jnp.max(s[f], axis=0, keepdims=True)
                m_old[f] = m.get((branch, f))
                m_new[f] = tile_max if m_old[f] is None else jnp.maximum(m_old[f], tile_max)
            if i + 1 < len(jobs):
                zeros = [(m_new[f][:, 0:LANES] * 0.0).astype(BF16) for f in range(2)]
                s_next = scores(jobs[i + 1], zeros)
            for f in range(2):
                pv = _dot(vt_ref[0, 0, :, keys], jnp.exp2(s[f] - m_new[f]).astype(BF16))
                if m_old[f] is None:
                    acc[branch, f] = pv
                else:
                    acc[branch, f] = jnp.exp2(m_old[f] - m_new[f]) * acc[branch, f] + pv
                m[branch, f] = m_new[f]
        return acc

    for n_past in range(ksa_ref.shape[2] // kt_sel):
        @pl.when(t0 // kt_sel == n_past)
        def _():
            acc = window_and_selected(n_past)
            for f in range(2):
                owin_ref[f] = normalized(acc["win", f])
                osel_ref[f] = normalized(acc["sel", f])
    o_win = [owin_ref[f] for f in range(2)]
    o_sel = [osel_ref[f] for f in range(2)]

    gates_t = gate_ref[0].T
    for k in range(HEADS_PER_GROUP // 2):
        heads = []
        for r in (2 * k, 2 * k + 1):
            cols = slice((r // 2) * tq, (r // 2 + 1) * tq)
            branches = (o_cmp[r], o_sel[r % 2][:, cols], o_win[r % 2][:, cols])
            out = jnp.zeros((HEAD_DIM, tq), F32)
            for br in range(N_BRANCHES):
                c = r * N_BRANCHES + br
                out = out + gates_t[c:c + 1, :] * branches[br]
            heads.append(out)
        o_ref[0, :, k * LANES:(k + 1) * LANES] = jnp.concatenate(heads, axis=0).T


def _attention(q, gates, kcmp, vcmpt, ksa, ksb, vst, kwa, kwb, vwt):
    B, T, _ = q.shape
    nc = kcmp.shape[2]
    tq = min(Q_TILE, T)
    assert T % tq == 0 and T >= WINDOW + tq and T // SEL_BLOCK <= MAX_SEL_BLOCKS
    topk = min(SEL_TOPK, T // SEL_BLOCK)
    k_spec = pl.BlockSpec((1, 1, T, LANES), lambda b, g, t: (b, g, 0, 0))
    vt_spec = pl.BlockSpec((1, 1, VT_ROWS, T), lambda b, g, t: (b, g, 0, 0))
    return pl.pallas_call(
        functools.partial(_attn_kernel, topk=topk),
        grid=(B, N_KV_GROUPS, T // tq),
        in_specs=[pl.BlockSpec((1, tq, GROUP_WIDTH), lambda b, g, t: (b, t, g)),
                  pl.BlockSpec((1, tq, LANES), lambda b, g, t: (b, t, g)),
                  pl.BlockSpec((1, 2, nc, LANES), lambda b, g, t: (b, g, 0, 0)),
                  pl.BlockSpec((1, HEAD_DIM, nc), lambda b, g, t: (b, g, 0)),
                  k_spec, k_spec, vt_spec, k_spec, k_spec, vt_spec],
        out_specs=pl.BlockSpec((1, tq, GROUP_WIDTH), lambda b, g, t: (b, t, g)),
        out_shape=jax.ShapeDtypeStruct((B, T, ATTN_WIDTH), F32),
        scratch_shapes=[pltpu.VMEM((tq // LANES, nc, LANES), F32),
                        pltpu.VMEM((MAX_SEL_BLOCKS, tq), F32),
                        pltpu.VMEM((2, HEAD_DIM, 2 * tq), F32),
                        pltpu.VMEM((2, HEAD_DIM, 2 * tq), F32)],
        compiler_params=pltpu.CompilerParams(dimension_semantics=("arbitrary",) * 3,
                                             vmem_limit_bytes=VMEM_LIMIT),
        name="nsa_attention",
    )(q, gates, kcmp, vcmpt, ksa, ksb, vst, kwa, kwb, vwt)


def _ffn_kernel(x_ref, pool_ref, attn_ref, nattn_ref, wo_ref, n2_ref, wg_ref, wu_ref, cw_ref, cb_ref,
                wd_ref, nf_ref, o_ref, prev_ref, *, final):
    t = pl.program_id(1)
    tt = x_ref.shape[1]
    attn = _rms(attn_ref[0], nattn_ref[...]).astype(BF16)
    x1 = (x_ref[0] + _dot(pool_ref[0].astype(BF16), wo_ref[0:POOL_WIDTH, :])
          + _dot(attn, wo_ref[POOL_WIDTH:, :]))
    h = _rms(x1, n2_ref[...]).astype(BF16)

    @pl.when(t == 0)
    def _():
        prev_ref[...] = jnp.zeros(prev_ref.shape, F32)

    row = lax.broadcasted_iota(jnp.int32, (tt, 1), 0)
    o_ref[0] = x1
    for c0 in range(0, D_FF, FF_CHUNK):
        cols = slice(c0, min(c0 + FF_CHUNK, D_FF))
        g = _dot(h, wg_ref[:, cols])
        p1 = prev_ref[7:8, cols]
        p2 = prev_ref[6:7, cols]
        g1 = jnp.where(row == 0, p1, pltpu.roll(g, 1, axis=0))
        g2 = jnp.where(row == 0, p2, jnp.where(row == 1, p1, pltpu.roll(g, 2, axis=0)))
        prev_ref[:, cols] = g[tt - 8:tt, :]
        gc = (g * cw_ref[2:3, cols] + g1 * cw_ref[1:2, cols] + g2 * cw_ref[0:1, cols]
              + cb_ref[:, cols])
        act = (jax.nn.silu(gc) * _dot(h, wu_ref[:, cols])).astype(BF16)
        o_ref[0] += _dot(act, wd_ref[cols, :])
    if final:
        o_ref[0] = _rms(o_ref[0], nf_ref[...])


def _ffn(x, pool, attn, nattn, wo, n2, wg, wu, cw, cb, wd, nf, final):
    B, T, D = x.shape
    tt = min(ROW_TILE, T)
    row3 = lambda w: pl.BlockSpec((1, tt, w), lambda b, t: (b, t, 0))
    full = lambda a: pl.BlockSpec(a.shape, lambda b, t: (0,) * a.ndim, pipeline_mode=pl.Buffered(1))
    return pl.pallas_call(
        functools.partial(_ffn_kernel, final=final),
        grid=(B, T // tt),
        in_specs=[row3(D), row3(POOL_WIDTH), row3(ATTN_WIDTH), full(nattn), full(wo), full(n2), full(wg),
                  full(wu), full(cw), full(cb), full(wd), full(nf)],
        out_specs=row3(D),
        out_shape=jax.ShapeDtypeStruct((B, T, D), F32),
        scratch_shapes=[pltpu.VMEM((8, D_FF), F32)],
        compiler_params=pltpu.CompilerParams(dimension_semantics=("arbitrary", "arbitrary"),
                                             vmem_limit_bytes=VMEM_LIMIT),
        name="outproj_ffn",
    )(x, pool, attn, nattn, wo, n2, wg, wu, cw, cb, wd, nf)


def _compress_weights(pe, w1, w2, k_forms):
    half = CMP_LEN // 2
    eye = jnp.eye(N_KV_GROUPS, dtype=F32)
    pe_flat = jnp.broadcast_to(pe.reshape(2, half, 1, HEAD_DIM), (2, half, N_KV_GROUPS, HEAD_DIM))
    pe_flat = pe_flat.reshape(2, half * KV_WIDTH)
    w1r = w1.reshape(2, half, HEAD_DIM, CMP_HIDDEN)
    w1s = jnp.einsum("hldc,ge->hlgdec", w1r, eye).reshape(2, half * KV_WIDTH, N_KV_GROUPS * CMP_HIDDEN)
    if k_forms:
        place = jnp.zeros((2, HEAD_DIM, LANES), F32)
        place = place.at[0, :, :HEAD_DIM].set(jnp.eye(HEAD_DIM)).at[1, :, HEAD_DIM:].set(jnp.eye(HEAD_DIM))
        w2s = jnp.einsum("cd,ge,fdm->gcefm", w2, eye, place)
        w2s = w2s.reshape(N_KV_GROUPS * CMP_HIDDEN, N_KV_GROUPS * 2 * LANES)
    else:
        w2s = jnp.einsum("cd,ge->gced", w2, eye).reshape(N_KV_GROUPS * CMP_HIDDEN, KV_WIDTH)
    return pe_flat, w1s.astype(BF16), w2s.astype(BF16)


def kernel(x, norm1, w_in, w_pool, s_pool, cmp_pe_k, cmp_w1_k, cmp_w2_k, cmp_pe_v, cmp_w1_v, cmp_w2_v,
           norm_pool_out, norm_attn_out, w_out, norm2, w_gate, w_up, conv_w, conv_b, w_down, norm_f):
    B, T, D = x.shape
    depth = w_in.shape[0]
    nc = T // CMP_STRIDE
    row = lambda v: v.reshape(1, -1)
    n_gate = HEADS_PER_GROUP * N_BRANCHES
    for l in range(depth):
        w = w_in[l].astype(BF16)
        o_kv = POOL_WIDTH + ATTN_WIDTH
        o_g = o_kv + 6 * KV_WIDTH
        wg = jnp.zeros((D, N_KV_GROUPS * LANES), BF16)
        for g in range(N_KV_GROUPS):
            wg = wg.at[:, g * LANES:g * LANES + n_gate].set(w[:, o_g + g * n_gate:o_g + (g + 1) * n_gate])
        pool, q, kc, vc, gates, ksa, ksb, vst, kwa, kwb, vwt = _inproj(
            x, row(norm1[l]), w[:, :POOL_WIDTH], w[:, POOL_WIDTH:o_kv], w[:, o_kv:o_g], wg,
            w_pool[l].astype(BF16), row(s_pool[l]), row(norm_pool_out[l]))
        pek, w1k, w2k = _compress_weights(cmp_pe_k[l], cmp_w1_k[l], cmp_w2_k[l], True)
        pev, w1v, w2v = _compress_weights(cmp_pe_v[l], cmp_w1_v[l], cmp_w2_v[l], False)
        kcmp, vcmpt = _compress(kc.reshape(B, nc, CMP_STRIDE * KV_WIDTH), vc.reshape(B, nc, CMP_STRIDE * KV_WIDTH),
                                pek, pev, w1k, w1v, w2k, w2v)
        attn = _attention(q, gates, kcmp, vcmpt, ksa, ksb, vst, kwa, kwb, vwt)
        x = _ffn(x, pool, attn, row(norm_attn_out[l]), w_out[l].astype(BF16), row(norm2[l]),
                 w_gate[l].astype(BF16), w_up[l].astype(BF16), conv_w[l], row(conv_b[l]),
                 w_down[l].astype(BF16), row(norm_f), final=(l == depth - 1))
    return x
```
